```python
import jax, jax.numpy as jnp
from jax import lax
import numpy as np

D_MODEL = 2048
BATCH = 4
SEQ = 4096
DEPTH = 4
DEC_BATCH = 8
DEC_SEQ = 16
PAST_LEN = 4096

CHUNK = 64
MA_HEADS = 8
MA_HEAD_DIM = 256
MA_WIDTH = MA_HEADS * MA_HEAD_DIM
CONV_W = 4
HB_HEADS = 16
HB_KEY_DIM = 128
HB_VAL_DIM = 128
HB_WIDTH = HB_HEADS * HB_KEY_DIM
HB_VWIDTH = HB_HEADS * HB_VAL_DIM
D_FF = 4 * D_MODEL
N_IN = 4 * MA_WIDTH + 2 * HB_WIDTH + 2 * HB_VWIDTH + 2 * D_MODEL + 2 * MA_HEADS
EPS = 1e-6
NEG = -1e30

kernel_name = 'hybrid_mlstm_hgrn2_streaming_step'


def rmsnorm(x, g):
    xf = x.astype(jnp.float32)
    y = xf * lax.rsqrt(jnp.mean(xf * xf, axis=-1, keepdims=True) + EPS)
    return (y * g.astype(jnp.float32)).astype(x.dtype)


def head_rmsnorm(h, g):
    y = h * lax.rsqrt(jnp.mean(h * h, axis=-1, keepdims=True) + EPS)
    return y.reshape(h.shape[:2] + (-1,)) * g.astype(jnp.float32)


def to_chunks(a, L):
    B, T = a.shape[:2]
    return jnp.swapaxes(a.reshape((B, T // L, L) + a.shape[2:]), 0, 1)


def from_chunks(a):
    a = jnp.swapaxes(a, 0, 1)
    return a.reshape((a.shape[0], a.shape[1] * a.shape[2]) + a.shape[3:])


def causal_dwconv(buf, u, w, b):
    full = jnp.concatenate([buf.astype(jnp.float32), u], axis=1)
    T = u.shape[1]
    wf = w.astype(jnp.float32)
    y = b.astype(jnp.float32) + full[:, 0:T] * wf[0]
    for j in range(1, CONV_W):
        y = y + full[:, j:j + T] * wf[j]
    return y, full[:, -(CONV_W - 1):]


def mlstm_step(carry, inp):
    c_mat, n_vec, m_run = carry
    q, k, v, ig, lf = inp
    L = q.shape[1]
    mask = jnp.tril(jnp.ones((L, L), dtype=bool))
    b = jnp.swapaxes(jnp.cumsum(lf, axis=1), 1, 2)
    igt = jnp.swapaxes(ig, 1, 2)
    log_d = jnp.where(mask, b[..., :, None] - b[..., None, :] + igt[..., None, :], NEG)
    log_inter = b + m_run[..., None]
    m_t = jnp.maximum(log_inter, jnp.max(log_d, axis=-1))
    d = jnp.exp(log_d - m_t[..., None])
    w_inter = jnp.exp(log_inter - m_t)
    s = jnp.einsum('blhd,bshd->bhls', q, k) * d
    num = (jnp.einsum('bhls,bshv->blhv', s, v)
           + jnp.swapaxes(w_inter, 1, 2)[..., None] * jnp.einsum('blhd,bhdv->blhv', q, c_mat))
    qn = jnp.sum(s, axis=-1) + w_inter * jnp.einsum('blhd,bhd->bhl', q, n_vec)
    denom = jnp.maximum(jnp.abs(qn), jnp.exp(-m_t))
    h = num / jnp.swapaxes(denom, 1, 2)[..., None]
    w_last = jnp.swapaxes(d[..., -1, :], 1, 2)[..., None]
    decay = w_inter[..., -1]
    c_new = decay[..., None, None] * c_mat + jnp.einsum('bshd,bshv->bhdv', k * w_last, v)
    n_new = decay[..., None] * n_vec + jnp.sum(k * w_last, axis=1)
    return (c_new, n_new, m_t[..., -1]), h


def mlstm_scan(q, k, v, ig, lf, c0, n0, m0):
    L = min(CHUNK, q.shape[1])
    xs = (to_chunks(q, L), to_chunks(k, L), to_chunks(v, L), to_chunks(ig, L), to_chunks(lf, L))
    init = (c0.astype(jnp.float32), n0.astype(jnp.float32), m0.astype(jnp.float32))
    (c1, n1, m1), h = lax.scan(mlstm_step, init, xs)
    return from_chunks(h), c1, n1, m1


def hgrn_step(s_mat, inp):
    q, k, v, lf = inp
    L = q.shape[1]
    mask = jnp.tril(jnp.ones((L, L), dtype=bool))[:, :, None, None]
    b = jnp.cumsum(lf, axis=1)
    decay = jnp.exp(jnp.where(mask, b[:, :, None] - b[:, None, :], NEG))
    a = jnp.einsum('btshd,bshd->bhts', decay * q[:, :, None], k)
    o = jnp.einsum('bhts,bshv->bthv', a, v) + jnp.einsum('bthd,bhdv->bthv', q * jnp.exp(b), s_mat)
    b_last = b[:, -1]
    s_new = (jnp.exp(b_last)[..., None] * s_mat
             + jnp.einsum('bshd,bshv->bhdv', k * jnp.exp(b_last[:, None] - b), v))
    return s_new, o


def hgrn_scan(q, k, v, lf, s0):
    L = min(CHUNK, q.shape[1])
    xs = (to_chunks(q, L), to_chunks(k, L), to_chunks(v, L), to_chunks(lf, L))
    s1, o = lax.scan(hgrn_step, s0.astype(jnp.float32), xs)
    return from_chunks(o), s1


def mixer_block(h, conv_buf, c0, n0, m0, s0, lb, w_in, b_in, conv_w, conv_b,
                ma_norm, hb_norm, w_br_a, w_br_b, w_o):
    B, T, _ = h.shape
    dt = h.dtype
    z = (h @ w_in + b_in).astype(jnp.float32)
    cuts = [int(i) for i in np.cumsum([2 * MA_WIDTH, MA_WIDTH, MA_WIDTH, HB_WIDTH, HB_WIDTH,
                                      HB_VWIDTH, HB_VWIDTH, D_MODEL, D_MODEL, MA_HEADS])]
    qk_a, v_a, o_a, q_b, f_b, i_b, og_b, g_a, g_b, ig_a, fg_a = jnp.split(z, cuts, axis=-1)
    qk_a, new_buf = causal_dwconv(conv_buf, qk_a, conv_w, conv_b)
    q_a, k_a = jnp.split(jax.nn.silu(qk_a), 2, axis=-1)
    q_a = q_a.reshape(B, T, MA_HEADS, MA_HEAD_DIM)
    k_a = k_a.reshape(B, T, MA_HEADS, MA_HEAD_DIM) * (MA_HEAD_DIM ** -0.5)
    v_a = v_a.reshape(B, T, MA_HEADS, MA_HEAD_DIM)
    h_a, c1, n1, m1 = mlstm_scan(q_a, k_a, v_a, ig_a, jax.nn.log_sigmoid(fg_a), c0, n0, m0)
    y_a = head_rmsnorm(h_a, ma_norm) * jax.nn.sigmoid(o_a)
    lb = lb.astype(jnp.float32)
    f_gate = lb + (1.0 - lb) * jax.nn.sigmoid(f_b)
    log_f = jnp.log(f_gate)
    k_b = (1.0 - lb) * jax.nn.sigmoid(-f_b)
    q_b = jax.nn.silu(q_b)
    hshape = (B, T, HB_HEADS, HB_KEY_DIM)
    h_b, s1 = hgrn_scan(q_b.reshape(hshape), k_b.reshape(hshape),
                        i_b.reshape(B, T, HB_HEADS, HB_VAL_DIM), log_f.reshape(hshape), s0)
    y_b = head_rmsnorm(h_b, hb_norm) * jax.nn.sigmoid(og_b)
    merged = (jax.nn.sigmoid(g_a) * (y_a.astype(dt) @ w_br_a)
              + jax.nn.sigmoid(g_b) * (y_b.astype(dt) @ w_br_b))
    out = merged.astype(dt) @ w_o
    return out, new_buf, c1, n1, m1, s1


def trunk(x, c, conv_c, st_c, st_n, st_m, st_s, lb_all, ada_w, ada_b, norm1_g, norm2_g,
          w_in, b_in, conv_w, conv_b, ma_norm, hb_norm, w_br_a, w_br_b, w_o, w_up, w_down, final_g):
    cs = jax.nn.silu(c)
    bufs, cms, nvs, mrs, sms = [], [], [], [], []
    for l in range(DEPTH):
        mod = (cs @ ada_w[l] + ada_b[l])[:, None, :]
        sh1, sc1, g1, sh2, sc2, g2 = jnp.split(mod, 6, axis=-1)
        h = rmsnorm(x, norm1_g[l]) * (1 + sc1) + sh1
        out, buf, c1, n1, m1, s1 = mixer_block(h, conv_c[l], st_c[l], st_n[l], st_m[l], st_s[l], lb_all[l],
                                               w_in[l], b_in[l], conv_w[l], conv_b[l], ma_norm[l], hb_norm[l],
                                               w_br_a[l], w_br_b[l], w_o[l])
        x = x + (g1 * out).astype(x.dtype)
        h = rmsnorm(x, norm2_g[l]) * (1 + sc2) + sh2
        u = jnp.square(jax.nn.relu(h @ w_up[l]))
        x = x + (g2 * (u @ w_down[l])).astype(x.dtype)
        bufs.append(buf); cms.append(c1); nvs.append(n1); mrs.append(m1); sms.append(s1)
    y = rmsnorm(x, final_g)
    return y, jnp.stack(bufs), jnp.stack(cms), jnp.stack(nvs), jnp.stack(mrs), jnp.stack(sms)


def setup_inputs(seed: int = 0) -> dict:
    key = jax.random.key(seed)
    ks = jax.random.split(key, 32)
    f32 = jnp.float32

    def nrm(k, shape, s):
        return jax.random.normal(k, shape, f32) * s

    D = D_MODEL
    b_in = nrm(ks[13], (DEPTH, N_IN), 0.02)
    fg_off = jax.random.uniform(ks[14], (DEPTH, MA_HEADS), f32, minval=3.0, maxval=6.0)
    b_in = b_in.at[:, N_IN - MA_HEADS:].add(fg_off)
    return {
        'x_prompt': nrm(ks[0], (BATCH, SEQ, D), 1.0),
        'x_sample': nrm(ks[1], (DEC_BATCH, DEC_SEQ, D), 1.0),
        'cache_conv': nrm(ks[2], (DEPTH, DEC_BATCH, CONV_W - 1, 2 * MA_WIDTH), 1.0),
        'state_mlstm_C': nrm(ks[3], (DEPTH, DEC_BATCH, MA_HEADS, MA_HEAD_DIM, MA_HEAD_DIM), 0.1),
        'state_mlstm_n': nrm(ks[4], (DEPTH, DEC_BATCH, MA_HEADS, MA_HEAD_DIM), 0.1),
        'state_mlstm_m': nrm(ks[5], (DEPTH, DEC_BATCH, MA_HEADS), 1.0),
        'state_hgrn': nrm(ks[6], (DEPTH, DEC_BATCH, HB_HEADS, HB_KEY_DIM, HB_VAL_DIM), 0.3),
        'c_prompt': nrm(ks[7], (BATCH, D), 1.0),
        'c_sample': nrm(ks[8], (DEC_BATCH, D), 1.0),
        'ada_w': nrm(ks[9], (DEPTH, D, 6 * D), 0.5 * D ** -0.5),
        'ada_b': nrm(ks[10], (DEPTH, 6 * D), 0.02),
        'norm1_g': 1.0 + nrm(ks[11], (DEPTH, D), 0.02),
        'norm2_g': 1.0 + nrm(ks[12], (DEPTH, D), 0.02),
        'w_in': nrm(ks[15], (DEPTH, D, N_IN), D ** -0.5),
        'b_in': b_in,
        'conv_w': nrm(ks[16], (DEPTH, CONV_W, 2 * MA_WIDTH), 0.5),
        'conv_b': nrm(ks[17], (DEPTH, 2 * MA_WIDTH), 0.02),
        'ma_norm': 1.0 + nrm(ks[18], (DEPTH, MA_WIDTH), 0.02),
        'hgrn_lb_raw': nrm(ks[19], (DEPTH, HB_WIDTH), 1.0),
        'hb_norm': 1.0 + nrm(ks[20], (DEPTH, HB_VWIDTH), 0.02),
        'w_br_a': nrm(ks[21], (DEPTH, MA_WIDTH, D), MA_WIDTH ** -0.5),
        'w_br_b': nrm(ks[22], (DEPTH, HB_VWIDTH, D), HB_VWIDTH ** -0.5),
        'w_o': nrm(ks[23], (DEPTH, D, D), D ** -0.5),
        'w_up': nrm(ks[24], (DEPTH, D, D_FF), D ** -0.5),
        'w_down': nrm(ks[25], (DEPTH, D_FF, D), D_FF ** -0.5),
        'final_g': 1.0 + nrm(ks[26], (D,), 0.02),
    }


def reference(x_prompt, x_sample, cache_conv, state_mlstm_C, state_mlstm_n, state_mlstm_m, state_hgrn,
              c_prompt, c_sample, ada_w, ada_b, norm1_g, norm2_g, w_in, b_in, conv_w, conv_b, ma_norm,
              hgrn_lb_raw, hb_norm, w_br_a, w_br_b, w_o, w_up, w_down, final_g):
    lb_sm = jax.nn.softmax(hgrn_lb_raw.astype(jnp.float32), axis=0)
    lb_all = jnp.cumsum(lb_sm, axis=0) - lb_sm[0]
    weights = (ada_w, ada_b, norm1_g, norm2_g, w_in, b_in, conv_w, conv_b, ma_norm, hb_norm,
               w_br_a, w_br_b, w_o, w_up, w_down, final_g)
    bp = x_prompt.shape[0]
    f32 = jnp.float32
    z_conv = jnp.zeros((DEPTH, bp, CONV_W - 1, 2 * MA_WIDTH), f32)
    z_c = jnp.zeros((DEPTH, bp, MA_HEADS, MA_HEAD_DIM, MA_HEAD_DIM), f32)
    z_n = jnp.zeros((DEPTH, bp, MA_HEADS, MA_HEAD_DIM), f32)
    z_m = jnp.zeros((DEPTH, bp, MA_HEADS), f32)
    z_s = jnp.zeros((DEPTH, bp, HB_HEADS, HB_KEY_DIM, HB_VAL_DIM), f32)
    y_prompt, conv_p, c_p, n_p, m_p, s_p = trunk(x_prompt, c_prompt, z_conv, z_c, z_n, z_m, z_s,
                                                 lb_all, *weights)
    y_sample, conv_s, c_s, n_s, m_s, s_s = trunk(x_sample, c_sample, cache_conv, state_mlstm_C,
                                                 state_mlstm_n, state_mlstm_m, state_hgrn,
                                                 lb_all, *weights)
    return (y_prompt, y_sample, conv_p, c_p, n_p, m_p, s_p, conv_s, c_s, n_s, m_s, s_s)
```

```python
import functools

import jax
import jax.numpy as jnp
from jax import lax
from jax.experimental import pallas as pl
from jax.experimental.pallas import tpu as pltpu

EPS = 1e-6
NEG = -1e30
CHUNK = 64
MA_HEADS = 8
MA_HEAD_DIM = 256
MA_WIDTH = MA_HEADS * MA_HEAD_DIM
CONV_W = 4
HB_HEADS = 16
HB_DIM = 128
HB_WIDTH = HB_HEADS * HB_DIM
GATE_PAD = 128
CONV_PAD = 8
HGRN_SAFE_EXP = 80.0
VMEM_LIMIT = 56 * 1024 * 1024

F32 = jnp.float32
BF16 = jnp.bfloat16

_NT = (((1,), (1,)), ((), ()))
_TN = (((0,), (0,)), ((), ()))


def _params(*sem):
    return pltpu.CompilerParams(dimension_semantics=sem, vmem_limit_bytes=VMEM_LIMIT)


def _sigmoid(x):
    return 1.0 / (1.0 + jnp.exp(-x))


def _log_sigmoid(x):
    return jnp.minimum(x, 0.0) - jnp.log1p(jnp.exp(-jnp.abs(x)))


def _dot(a, b):
    return jnp.dot(a, b, preferred_element_type=F32)


def _lb_kernel(raw_ref, o_ref):
    x = raw_ref[...]
    e = jnp.exp(x - jnp.max(x, axis=0, keepdims=True))
    sm = e / jnp.sum(e, axis=0, keepdims=True)
    depth = x.shape[0]
    acc = sm[0:1, :]
    o_ref[0:1, :] = acc - sm[0:1, :]
    for l in range(1, depth):
        acc = acc + sm[l:l + 1, :]
        o_ref[l:l + 1, :] = acc - sm[0:1, :]


def _lower_bounds(raw):
    return pl.pallas_call(
        _lb_kernel, out_shape=jax.ShapeDtypeStruct(raw.shape, F32), name="hgrn_lower_bounds")(raw)


def _ada_kernel(c_ref, w_ref, b_ref, o_ref):
    c = c_ref[...]
    cs = (c * _sigmoid(c)).astype(BF16)
    o_ref[...] = _dot(cs, w_ref[...].astype(BF16)) + b_ref[...]


def _modulation(c_all, ada_w, ada_b, tn=1024):
    depth, d, n6 = ada_w.shape
    rows = c_all.shape[0]
    return pl.pallas_call(
        _ada_kernel,
        grid=(depth, n6 // tn),
        in_specs=[pl.BlockSpec((rows, d), lambda l, j: (0, 0)),
                  pl.BlockSpec((None, d, tn), lambda l, j: (l, 0, j)),
                  pl.BlockSpec((None, 1, tn), lambda l, j: (l, 0, j))],
        out_specs=pl.BlockSpec((None, rows, tn), lambda l, j: (l, 0, j)),
        out_shape=jax.ShapeDtypeStruct((depth, rows, n6), F32),
        compiler_params=_params("parallel", "parallel"),
        name="ada_modulation",
    )(c_all, ada_w, ada_b.reshape(depth, 1, n6))


def _norm_mod_rows(x_ref, g_ref, sc_ref, sh_ref, h_scr):
    tm = x_ref.shape[0]
    rows = min(tm, 256)

    def body(r, carry):
        sl = pl.ds(pl.multiple_of(r * rows, rows), rows)
        x = x_ref[sl, :]
        y = x * lax.rsqrt(jnp.mean(x * x, axis=-1, keepdims=True) + EPS) * g_ref[...]
        sc = sc_ref[...] if sc_ref.shape[0] == 1 else sc_ref[sl, :]
        sh = sh_ref[...] if sh_ref.shape[0] == 1 else sh_ref[sl, :]
        h_scr[sl, :] = (y * (1.0 + sc) + sh).astype(BF16)
        return carry

    lax.fori_loop(0, tm // rows, body, 0)


def _mod_spec(mod, k, d, tm, tiles_per_stream):
    if mod.ndim == 3:
        return pl.BlockSpec((None, 1, d), lambda i, j: (i // tiles_per_stream, 0, k))
    return pl.BlockSpec((tm, d), lambda i, j: (i, k))


def _mm_in_kernel(x_ref, g_ref, sc_ref, sh_ref, w_ref, b_ref, wg_ref, bg_ref, z_ref, zg_ref, h_scr):
    @pl.when(pl.program_id(1) == 0)
    def _():
        _norm_mod_rows(x_ref, g_ref, sc_ref, sh_ref, h_scr)
        zg_ref[...] = _dot(h_scr[...], wg_ref[...]) + bg_ref[...]

    z_ref[...] = _dot(h_scr[...], w_ref[...]) + b_ref[...]


def _mm_in(x, g, mod, w, b, wg, bg, rows_per_stream, tm=1024, tn=1024):
    n, d = x.shape
    nz = w.shape[1]
    tm = min(tm, n if mod.ndim == 2 else rows_per_stream)
    tps = max(rows_per_stream // tm, 1)
    return pl.pallas_call(
        _mm_in_kernel,
        grid=(n // tm, nz // tn),
        in_specs=[pl.BlockSpec((tm, d), lambda i, j: (i, 0)),
                  pl.BlockSpec((1, d), lambda i, j: (0, 0)),
                  _mod_spec(mod, 1, d, tm, tps),
                  _mod_spec(mod, 0, d, tm, tps),
                  pl.BlockSpec((d, tn), lambda i, j: (0, j)),
                  pl.BlockSpec((1, tn), lambda i, j: (0, j)),
                  pl.BlockSpec((d, GATE_PAD), lambda i, j: (0, 0)),
                  pl.BlockSpec((1, GATE_PAD), lambda i, j: (0, 0))],
        out_specs=[pl.BlockSpec((tm, tn), lambda i, j: (i, j)),
                   pl.BlockSpec((tm, GATE_PAD), lambda i, j: (i, 0))],
        out_shape=[jax.ShapeDtypeStruct((n, nz), F32),
                   jax.ShapeDtypeStruct((n, GATE_PAD), F32)],
        scratch_shapes=[pltpu.VMEM((tm, d), BF16)],
        compiler_params=_params("parallel", "arbitrary"),
        name="norm_in_proj",
    )(x, g, mod, mod, w, b, wg, bg)


def _mlstm_kernel(zq_ref, zk_ref, zv_ref, zo_ref, zg_ref, gt_ref, cw_ref, cb_ref, nrm_ref,
                  conv0_ref, c0_ref, n0_ref, m0_ref,
                  y_ref, c_ref, n_ref, m_ref, xq_scr, xk_scr):
    L = zq_ref.shape[0]
    hd = MA_HEAD_DIM

    @pl.when(pl.program_id(1) == 0)
    def _():
        c_ref[...] = c0_ref[...]
        n_ref[...] = n0_ref[...]
        m_ref[...] = m0_ref[...]
        xq_scr[0:CONV_PAD, :] = conv0_ref[0, :, 0:MA_WIDTH]
        xk_scr[0:CONV_PAD, :] = conv0_ref[0, :, MA_WIDTH:2 * MA_WIDTH]

    xq_scr[CONV_PAD:CONV_PAD + L, :] = zq_ref[...]
    xk_scr[CONV_PAD:CONV_PAD + L, :] = zk_ref[...]

    ti = lax.broadcasted_iota(jnp.int32, (L, L), 0)
    si = lax.broadcasted_iota(jnp.int32, (L, L), 1)
    causal = si <= ti
    gates_t = gt_ref[0, 0]
    first = CONV_PAD - (CONV_W - 1)

    for h in range(MA_HEADS):
        lo = h * hd

        def conv_silu(xs_ref, off):
            acc = cb_ref[:, off + lo:off + lo + hd] + (
                xs_ref[first:first + L, lo:lo + hd] * cw_ref[0:1, off + lo:off + lo + hd])
            for j in range(1, CONV_W):
                acc = acc + (xs_ref[first + j:first + j + L, lo:lo + hd]
                             * cw_ref[j:j + 1, off + lo:off + lo + hd])
            return acc * _sigmoid(acc)

        q = conv_silu(xq_scr, 0)
        k = conv_silu(xk_scr, MA_WIDTH) * (MA_HEAD_DIM ** -0.5)
        v = zv_ref[:, lo:lo + hd]

        ig_col = zg_ref[:, h:h + 1]
        lf_col = _log_sigmoid(zg_ref[:, MA_HEADS + h:MA_HEADS + h + 1])
        ig_row = gates_t[h:h + 1, :]
        lf_row = _log_sigmoid(gates_t[MA_HEADS + h:MA_HEADS + h + 1, :])

        b_col = jnp.sum(jnp.where(causal, lf_row, 0.0), axis=1, keepdims=True)
        b_row = jnp.sum(jnp.where(ti <= si, lf_col, 0.0), axis=0, keepdims=True)
        log_d = jnp.where(causal, b_col - b_row + ig_row, NEG)
        m_run = m_ref[0, h:h + 1, 0:1]
        log_inter = b_col + m_run
        m_t = jnp.maximum(log_inter, jnp.max(log_d, axis=1, keepdims=True))
        d = jnp.exp(log_d - m_t)
        w_inter = jnp.exp(log_inter - m_t)

        qb = q.astype(BF16)
        kb = k.astype(BF16)
        vb = v.astype(BF16)
        s = lax.dot_general(qb, kb, _NT, preferred_element_type=F32) * d
        c_mat = c_ref[0, h]
        n_row = n_ref[0, h:h + 1, :]
        num = _dot(s.astype(BF16), vb) + w_inter * _dot(qb, c_mat.astype(BF16))
        qn = (jnp.sum(s, axis=1, keepdims=True)
              + w_inter * jnp.sum(q * n_row, axis=1, keepdims=True))
        denom = jnp.maximum(jnp.abs(qn), jnp.exp(-m_t))
        hh = num / denom

        b_last = b_col[L - 1:L, :]
        m_last = m_t[L - 1:L, :]
        w_last = jnp.exp(b_last - b_col + ig_col - m_last)
        decay = w_inter[L - 1:L, :]
        kw = k * w_last
        c_ref[0, h] = decay * c_mat + lax.dot_general(kw.astype(BF16), vb, _TN,
                                                      preferred_element_type=F32)
        n_ref[0, h:h + 1, :] = decay * n_row + jnp.sum(kw, axis=0, keepdims=True)
        m_ref[0, h:h + 1, :] = jnp.broadcast_to(m_last, (1, m_ref.shape[2]))

        y = hh * lax.rsqrt(jnp.mean(hh * hh, axis=-1, keepdims=True) + EPS)
        y = y * nrm_ref[:, lo:lo + hd] * _sigmoid(zo_ref[:, lo:lo + hd])
        y_ref[:, lo:lo + hd] = y.astype(BF16)

    xq_scr[0:CONV_PAD, :] = xq_scr[L:L + CONV_PAD, :]
    xk_scr[0:CONV_PAD, :] = xk_scr[L:L + CONV_PAD, :]


def _mlstm(z, zg, gates_t, conv_w, conv_b, ma_norm, conv0, c0, n0, m0, batch, seq, L):
    nc = seq // L
    w = MA_WIDTH
    blk = lambda col: pl.BlockSpec((L, w), lambda b, c: (b * nc + c, col))
    st4 = pl.BlockSpec((1, MA_HEADS, MA_HEAD_DIM, MA_HEAD_DIM), lambda b, c: (b, 0, 0, 0))
    st3 = pl.BlockSpec((1, MA_HEADS, MA_HEAD_DIM), lambda b, c: (b, 0, 0))
    stm = pl.BlockSpec((1, MA_HEADS, GATE_PAD), lambda b, c: (b, 0, 0))
    return pl.pallas_call(
        _mlstm_kernel,
        grid=(batch, nc),
        in_specs=[blk(0), blk(1), blk(2), blk(3),
                  pl.BlockSpec((L, GATE_PAD), lambda b, c: (b * nc + c, 0)),
                  pl.BlockSpec((1, 1, 2 * MA_HEADS, L), lambda b, c: (b, c, 0, 0)),
                  pl.BlockSpec((CONV_W, 2 * w), lambda b, c: (0, 0)),
                  pl.BlockSpec((1, 2 * w), lambda b, c: (0, 0)),
                  pl.BlockSpec((1, w), lambda b, c: (0, 0)),
                  pl.BlockSpec((1, CONV_PAD, 2 * w), lambda b, c: (b, 0, 0)),
                  st4, st3, stm],
        out_specs=[pl.BlockSpec((L, w), lambda b, c: (b * nc + c, 0)), st4, st3, stm],
        out_shape=[jax.ShapeDtypeStruct((batch * seq, w), BF16),
                   jax.ShapeDtypeStruct(c0.shape, F32),
                   jax.ShapeDtypeStruct(n0.shape, F32),
                   jax.ShapeDtypeStruct(m0.shape, F32)],
        scratch_shapes=[pltpu.VMEM((CONV_PAD + L, w), F32), pltpu.VMEM((CONV_PAD + L, w), F32)],
        compiler_params=_params("parallel", "arbitrary"),
        name="mlstm_scan",
    )(z, z, z, z, zg, gates_t, conv_w, conv_b, ma_norm, conv0, c0, n0, m0)


def _hgrn_kernel(zq_ref, zf_ref, zi_ref, zog_ref, lb_ref, nrm_ref, s0_ref,
                 y_ref, s_ref, st_scr, b_scr, k_scr, q_scr, o_scr):
    L = zq_ref.shape[0]
    hd = HB_DIM
    nlast = pl.num_programs(1) - 1

    @pl.when(pl.program_id(1) == 0)
    def _():
        for h in range(HB_HEADS):
            st_scr[h] = s0_ref[0, h].T

    lb = lb_ref[...]
    fb = zf_ref[...]
    log_f = jnp.log(lb + (1.0 - lb) * _sigmoid(fb))
    k = (1.0 - lb) * _sigmoid(-fb)
    qr = zq_ref[...]
    q = qr * _sigmoid(qr)

    ti = lax.broadcasted_iota(jnp.int32, (L, L), 0)
    si = lax.broadcasted_iota(jnp.int32, (L, L), 1)
    causal = si <= ti
    tril = causal.astype(BF16)
    f_hi = log_f.astype(BF16)
    r1 = log_f - f_hi.astype(F32)
    f_mid = r1.astype(BF16)
    f_lo = (r1 - f_mid.astype(F32)).astype(BF16)
    b = _dot(tril, f_hi) + _dot(tril, f_mid) + _dot(tril, f_lo)

    mid = L // 2
    r = b[mid - 1:mid, :]
    b_last = b[L - 1:L, :]
    e_last = jnp.exp(b_last)
    qs = (q * jnp.exp(b)).astype(BF16)
    ks = (k * jnp.exp(b_last - b)).astype(BF16)
    vb = zi_ref[...].astype(BF16)

    for h in range(HB_HEADS):
        sl = slice(h * hd, (h + 1) * hd)
        st = st_scr[h]
        o_scr[:, sl] = lax.dot_general(qs[:, sl], st.astype(BF16), _NT, preferred_element_type=F32)
        st_scr[h] = e_last[:, sl] * st + lax.dot_general(vb[:, sl], ks[:, sl], _TN,
                                                         preferred_element_type=F32)

    spread = jnp.max(jnp.maximum(-r, r - b_last))
    safe = spread <= HGRN_SAFE_EXP

    @pl.when(safe)
    def _():
        qt = (q * jnp.exp(b - r)).astype(BF16)
        kt = (k * jnp.exp(r - b)).astype(BF16)
        for h in range(HB_HEADS):
            sl = slice(h * hd, (h + 1) * hd)
            a = lax.dot_general(qt[:, sl], kt[:, sl], _NT, preferred_element_type=F32)
            a = jnp.where(causal, a, 0.0)
            o_scr[:, sl] += _dot(a.astype(BF16), vb[:, sl])

    @pl.when(jnp.logical_not(safe))
    def _():
        b_scr[...] = b
        k_scr[...] = k
        q_scr[...] = q
        width = b.shape[1]
        head_of_lane = lax.broadcasted_iota(jnp.int32, (width, GATE_PAD), 0) // hd
        col = lax.broadcasted_iota(jnp.int32, (width, GATE_PAD), 1)
        gather = (head_of_lane == col).astype(BF16)
        head_of_lane_t = lax.broadcasted_iota(jnp.int32, (GATE_PAD, width), 1) // hd
        row = lax.broadcasted_iota(jnp.int32, (GATE_PAD, width), 0)
        scatter = (head_of_lane_t == row).astype(BF16)
        t_idx = lax.broadcasted_iota(jnp.int32, (L, GATE_PAD), 0)

        def body(s, carry):
            b_s = b_scr[pl.ds(s, 1), :]
            k_s = k_scr[pl.ds(s, 1), :]
            v_s = zi_ref[pl.ds(s, 1), :]
            e = jnp.exp(jnp.minimum(b_scr[...] - b_s, 0.0)) * q_scr[...] * k_s
            a = jnp.where(t_idx >= s, _dot(e.astype(BF16), gather), 0.0)
            o_scr[...] += _dot(a.astype(BF16), scatter) * v_s
            return carry

        lax.fori_loop(0, L, body, 0)

    for h in range(HB_HEADS):
        sl = slice(h * hd, (h + 1) * hd)
        o = o_scr[:, sl]
        y = o * lax.rsqrt(jnp.mean(o * o, axis=-1, keepdims=True) + EPS)
        y = y * nrm_ref[:, sl] * _sigmoid(zog_ref[:, sl])
        y_ref[:, sl] = y.astype(BF16)

    @pl.when(pl.program_id(1) == nlast)
    def _():
        for h in range(HB_HEADS):
            s_ref[0, h] = st_scr[h].T


def _hgrn(z, lb, hb_norm, s0, batch, seq, L):
    nc = seq // L
    w = HB_WIDTH
    blk = lambda col: pl.BlockSpec((L, w), lambda b, c: (b * nc + c, col))
    st = pl.BlockSpec((1, HB_HEADS, HB_DIM, HB_DIM), lambda b, c: (b, 0, 0, 0))
    return pl.pallas_call(
        _hgrn_kernel,
        grid=(batch, nc),
        in_specs=[blk(4), blk(5), blk(6), blk(7),
                  pl.BlockSpec((1, w), lambda b, c: (0, 0)),
                  pl.BlockSpec((1, w), lambda b, c: (0, 0)),
                  st],
        out_specs=[pl.BlockSpec((L, w), lambda b, c: (b * nc + c, 0)), st],
        out_shape=[jax.ShapeDtypeStruct((batch * seq, w), BF16),
                   jax.ShapeDtypeStruct(s0.shape, F32)],
        scratch_shapes=[pltpu.VMEM((HB_HEADS, HB_DIM, HB_DIM), F32),
                        pltpu.VMEM((L, w), F32), pltpu.VMEM((L, w), F32),
                        pltpu.VMEM((L, w), F32), pltpu.VMEM((L, w), F32)],
        compiler_params=_params("parallel", "arbitrary"),
        name="hgrn_scan",
    )(z, z, z, z, lb, hb_norm, s0)


def _mm_br_kernel(ya_ref, yb_ref, wa_ref, wb_ref, ga_ref, gb_ref, o_ref):
    ta = _dot(ya_ref[...], wa_ref[...])
    tb = _dot(yb_ref[...], wb_ref[...])
    o_ref[...] = (_sigmoid(ga_ref[...]) * ta + _sigmoid(gb_ref[...]) * tb).astype(BF16)


def _mm_branches(ya, yb, wa, wb, z, ga_col, gb_col, tm=1024, tn=512):
    n, kdim = ya.shape
    d = wa.shape[1]
    tm = min(tm, n)
    ga_blk, gb_blk = ga_col // tn, gb_col // tn
    return pl.pallas_call(
        _mm_br_kernel,
        grid=(n // tm, d // tn),
        in_specs=[pl.BlockSpec((tm, kdim), lambda i, j: (i, 0)),
                  pl.BlockSpec((tm, kdim), lambda i, j: (i, 0)),
                  pl.BlockSpec((kdim, tn), lambda i, j: (0, j)),
                  pl.BlockSpec((kdim, tn), lambda i, j: (0, j)),
                  pl.BlockSpec((tm, tn), lambda i, j: (i, ga_blk + j)),
                  pl.BlockSpec((tm, tn), lambda i, j: (i, gb_blk + j))],
        out_specs=pl.BlockSpec((tm, tn), lambda i, j: (i, j)),
        out_shape=jax.ShapeDtypeStruct((n, d), BF16),
        compiler_params=_params("parallel", "arbitrary"),
        name="branch_merge",
    )(ya, yb, wa, wb, z, z)


def _mm_res_kernel(a_ref, w_ref, x_ref, gt_ref, o_ref):
    o_ref[...] = x_ref[...] + gt_ref[...] * _dot(a_ref[...], w_ref[...])


def _mm_residual(a, w, x, mod, gate_chunk, rows_per_stream, tm=1024, tn=512):
    n, kdim = a.shape
    d = w.shape[1]
    tm = min(tm, n if mod.ndim == 2 else rows_per_stream)
    tps = max(rows_per_stream // tm, 1)
    nj = d // tn
    if mod.ndim == 3:
        gate_spec = pl.BlockSpec((None, 1, tn), lambda i, j: (i // tps, 0, gate_chunk * nj + j))
    else:
        gate_spec = pl.BlockSpec((tm, tn), lambda i, j: (i, gate_chunk * nj + j))
    return pl.pallas_call(
        _mm_res_kernel,
        grid=(n // tm, nj),
        in_specs=[pl.BlockSpec((tm, kdim), lambda i, j: (i, 0)),
                  pl.BlockSpec((kdim, tn), lambda i, j: (0, j)),
                  pl.BlockSpec((tm, tn), lambda i, j: (i, j)),
                  gate_spec],
        out_specs=pl.BlockSpec((tm, tn), lambda i, j: (i, j)),
        out_shape=jax.ShapeDtypeStruct((n, d), F32),
        compiler_params=_params("parallel", "arbitrary"),
        name="out_proj_residual",
    )(a, w, x, mod)


def _ffn_kernel(x_ref, g_ref, sc_ref, sh_ref, gt_ref, wu_ref, wd_ref, fg_ref, o_ref, h_scr, *, final):
    f = pl.program_id(1)

    @pl.when(f == 0)
    def _():
        _norm_mod_rows(x_ref, g_ref, sc_ref, sh_ref, h_scr)
        o_ref[...] = jnp.zeros_like(o_ref)

    u = jnp.square(jnp.maximum(_dot(h_scr[...], wu_ref[...]), 0.0))
    o_ref[...] += _dot(u.astype(BF16), wd_ref[...])

    @pl.when(f == pl.num_programs(1) - 1)
    def _():
        tm = x_ref.shape[0]
        rows = min(tm, 256)

        def body(r, carry):
            sl = pl.ds(pl.multiple_of(r * rows, rows), rows)
            gt = gt_ref[...] if gt_ref.shape[0] == 1 else gt_ref[sl, :]
            xn = x_ref[sl, :] + gt * o_ref[sl, :]
            if final:
                xn = xn * lax.rsqrt(jnp.mean(xn * xn, axis=-1, keepdims=True) + EPS) * fg_ref[...]
            o_ref[sl, :] = xn
            return carry

        lax.fori_loop(0, tm // rows, body, 0)


def _ffn(x, g, mod, wu, wd, final_g, final, rows_per_stream, tm=512, tf=512):
    n, d = x.shape
    dff = wu.shape[1]
    tm = min(tm, n if mod.ndim == 2 else rows_per_stream)
    tps = max(rows_per_stream // tm, 1)
    return pl.pallas_call(
        functools.partial(_ffn_kernel, final=final),
        grid=(n // tm, dff // tf),
        in_specs=[pl.BlockSpec((tm, d), lambda i, j: (i, 0)),
                  pl.BlockSpec((1, d), lambda i, j: (0, 0)),
                  _mod_spec(mod, 4, d, tm, tps),
                  _mod_spec(mod, 3, d, tm, tps),
                  _mod_spec(mod, 5, d, tm, tps),
                  pl.BlockSpec((d, tf), lambda i, j: (0, j)),
                  pl.BlockSpec((tf, d), lambda i, j: (j, 0)),
                  pl.BlockSpec((1, d), lambda i, j: (0, 0))],
        out_specs=pl.BlockSpec((tm, d), lambda i, j: (i, 0)),
        out_shape=jax.ShapeDtypeStruct((n, d), F32),
        scratch_shapes=[pltpu.VMEM((tm, d), BF16)],
        compiler_params=_params("parallel", "arbitrary"),
        name="ffn_residual",
    )(x, g, mod, mod, mod, wu, wd, final_g)


def _trunk(x, mod_all, per_token_mod, conv_c, st_c, st_n, st_m, st_s, lb_all, wts):
    batch, seq, d = x.shape
    depth = conv_c.shape[0]
    L = min(CHUNK, seq)
    nc = seq // L
    n = batch * seq
    xf = x.reshape(n, d)
    bufs, cms, nvs, mrs, sms = [], [], [], [], []
    for l in range(depth):
        mod = mod_all[l]
        if per_token_mod:
            mod = jnp.repeat(mod, seq, axis=0)
        else:
            mod = mod[:, None, :]
        z, zg = _mm_in(xf, wts["norm1_g"][l], mod, wts["w_in"][l], wts["b_in"][l],
                       wts["w_gate"][l], wts["b_gate"][l], seq)
        gates_t = jnp.swapaxes(zg[:, :2 * MA_HEADS].reshape(batch, nc, L, 2 * MA_HEADS), 2, 3)
        conv0 = jnp.pad(conv_c[l], ((0, 0), (CONV_PAD - (CONV_W - 1), 0), (0, 0)))
        m0 = jnp.broadcast_to(st_m[l][:, :, None], (batch, MA_HEADS, GATE_PAD))
        ya, c1, n1, m1 = _mlstm(z, zg, gates_t, wts["conv_w"][l], wts["conv_b"][l], wts["ma_norm"][l],
                                conv0, st_c[l], st_n[l], m0, batch, seq, L)
        yb, s1 = _hgrn(z, lb_all[l:l + 1], wts["hb_norm"][l], st_s[l], batch, seq, L)
        merged = _mm_branches(ya, yb, wts["w_br_a"][l], wts["w_br_b"][l], z,
                              4 * MA_WIDTH + 4 * HB_WIDTH, 4 * MA_WIDTH + 4 * HB_WIDTH + d)
        xf = _mm_residual(merged, wts["w_o"][l], xf, mod, 2, seq)
        xf = _ffn(xf, wts["norm2_g"][l], mod, wts["w_up"][l], wts["w_down"][l], wts["final_g"],
                  l == depth - 1, seq)
        bufs.append(z[:, :2 * MA_WIDTH].reshape(batch, seq, 2 * MA_WIDTH)[:, seq - (CONV_W - 1):, :])
        cms.append(c1)
        nvs.append(n1)
        mrs.append(m1[:, :, 0])
        sms.append(s1)
    return (xf.reshape(batch, seq, d), jnp.stack(bufs), jnp.stack(cms), jnp.stack(nvs),
            jnp.stack(mrs), jnp.stack(sms))


def kernel(x_prompt, x_sample, cache_conv, state_mlstm_C, state_mlstm_n, state_mlstm_m, state_hgrn,
           c_prompt, c_sample, ada_w, ada_b, norm1_g, norm2_g, w_in, b_in, conv_w, conv_b, ma_norm,
           hgrn_lb_raw, hb_norm, w_br_a, w_br_b, w_o, w_up, w_down, final_g):
    depth, d, n_in = w_in.shape
    bp, bs = x_prompt.shape[0], x_sample.shape[0]
    n_main = n_in - 2 * MA_HEADS
    gate_pad = GATE_PAD - 2 * MA_HEADS

    lb_all = _lower_bounds(hgrn_lb_raw)
    c_all = jnp.concatenate([c_prompt, c_sample], axis=0)
    c_rows = -(-c_all.shape[0] // 8) * 8
    c_all = jnp.pad(c_all, ((0, c_rows - c_all.shape[0]), (0, 0)))
    mod_all = _modulation(c_all, ada_w, ada_b)

    wts = {
        "norm1_g": norm1_g.reshape(depth, 1, d),
        "norm2_g": norm2_g.reshape(depth, 1, d),
        "w_in": w_in[:, :, :n_main].astype(BF16),
        "b_in": b_in[:, :n_main].reshape(depth, 1, n_main),
        "w_gate": jnp.pad(w_in[:, :, n_main:], ((0, 0), (0, 0), (0, gate_pad))).astype(BF16),
        "b_gate": jnp.pad(b_in[:, n_main:], ((0, 0), (0, gate_pad))).reshape(depth, 1, GATE_PAD),
        "conv_w": conv_w,
        "conv_b": conv_b.reshape(depth, 1, -1),
        "ma_norm": ma_norm.reshape(depth, 1, -1),
        "hb_norm": hb_norm.reshape(depth, 1, -1),
        "w_br_a": w_br_a.astype(BF16),
        "w_br_b": w_br_b.astype(BF16),
        "w_o": w_o.astype(BF16),
        "w_up": w_up.astype(BF16),
        "w_down": w_down.astype(BF16),
        "final_g": final_g.reshape(1, d),
    }

    f32 = jnp.float32
    z_conv = jnp.zeros((depth, bp, CONV_W - 1, 2 * MA_WIDTH), f32)
    z_c = jnp.zeros((depth, bp, MA_HEADS, MA_HEAD_DIM, MA_HEAD_DIM), f32)
    z_n = jnp.zeros((depth, bp, MA_HEADS, MA_HEAD_DIM), f32)
    z_m = jnp.zeros((depth, bp, MA_HEADS), f32)
    z_s = jnp.zeros((depth, bp, HB_HEADS, HB_DIM, HB_DIM), f32)

    out_p = _trunk(x_prompt, mod_all[:, :bp], False, z_conv, z_c, z_n, z_m, z_s, lb_all, wts)
    out_s = _trunk(x_sample, mod_all[:, bp:bp + bs], True, cache_conv, state_mlstm_C, state_mlstm_n,
                   state_mlstm_m, state_hgrn, lb_all, wts)
    y_p, conv_p, c_p, n_p, m_p, s_p = out_p
    y_s, conv_s, c_s, n_s, m_s, s_s = out_s
    return (y_p, y_s, conv_p, c_p, n_p, m_p, s_p, conv_s, c_s, n_s, m_s, s_s)
```

```python
import functools

import jax
import jax.numpy as jnp
from jax import lax
from jax.experimental import pallas as pl
from jax.experimental.pallas import tpu as pltpu

EPS = 1e-6
NEG = -1e30
MA_CHUNK = 256
HB_CHUNK = 64
MA_HEADS = 8
MA_HEAD_DIM = 256
MA_WIDTH = MA_HEADS * MA_HEAD_DIM
CONV_W = 4
HB_HEADS = 16
HB_DIM = 128
HB_WIDTH = HB_HEADS * HB_DIM
GATE_PAD = 128
CONV_PAD = 8
HGRN_SAFE_EXP = 80.0
VMEM_LIMIT = 56 * 1024 * 1024

F32 = jnp.float32
BF16 = jnp.bfloat16

_NT = (((1,), (1,)), ((), ()))
_TN = (((0,), (0,)), ((), ()))


def _params(*sem):
    return pltpu.CompilerParams(dimension_semantics=sem, vmem_limit_bytes=VMEM_LIMIT)


def _sigmoid(x):
    return 1.0 / (1.0 + jnp.exp(-x))


def _log_sigmoid(x):
    return jnp.minimum(x, 0.0) - jnp.log1p(jnp.exp(-jnp.abs(x)))


def _dot(a, b):
    return jnp.dot(a, b, preferred_element_type=F32)


def _lb_kernel(raw_ref, o_ref):
    x = raw_ref[...]
    e = jnp.exp(x - jnp.max(x, axis=0, keepdims=True))
    sm = e / jnp.sum(e, axis=0, keepdims=True)
    depth = x.shape[0]
    acc = sm[0:1, :]
    o_ref[0:1, :] = acc - sm[0:1, :]
    for l in range(1, depth):
        acc = acc + sm[l:l + 1, :]
        o_ref[l:l + 1, :] = acc - sm[0:1, :]


def _lower_bounds(raw):
    return pl.pallas_call(
        _lb_kernel, out_shape=jax.ShapeDtypeStruct(raw.shape, F32), name="hgrn_lower_bounds")(raw)


def _ada_kernel(c_ref, w_ref, b_ref, o_ref):
    c = c_ref[...]
    cs = (c * _sigmoid(c)).astype(BF16)
    o_ref[...] = _dot(cs, w_ref[...].astype(BF16)) + b_ref[...]


def _modulation(c_all, ada_w, ada_b, tn=1024):
    depth, d, n6 = ada_w.shape
    rows = c_all.shape[0]
    return pl.pallas_call(
        _ada_kernel,
        grid=(depth, n6 // tn),
        in_specs=[pl.BlockSpec((rows, d), lambda l, j: (0, 0)),
                  pl.BlockSpec((None, d, tn), lambda l, j: (l, 0, j)),
                  pl.BlockSpec((None, 1, tn), lambda l, j: (l, 0, j))],
        out_specs=pl.BlockSpec((None, rows, tn), lambda l, j: (l, 0, j)),
        out_shape=jax.ShapeDtypeStruct((depth, rows, n6), F32),
        compiler_params=_params("parallel", "parallel"),
        name="ada_modulation",
    )(c_all, ada_w, ada_b.reshape(depth, 1, n6))


def _norm_mod_rows(x_ref, g_ref, sc_ref, sh_ref, h_scr):
    tm = x_ref.shape[0]
    rows = min(tm, 256)

    def body(r, carry):
        sl = pl.ds(pl.multiple_of(r * rows, rows), rows)
        x = x_ref[sl, :]
        y = x * lax.rsqrt(jnp.mean(x * x, axis=-1, keepdims=True) + EPS) * g_ref[...]
        sc = sc_ref[...] if sc_ref.shape[0] == 1 else sc_ref[sl, :]
        sh = sh_ref[...] if sh_ref.shape[0] == 1 else sh_ref[sl, :]
        h_scr[sl, :] = (y * (1.0 + sc) + sh).astype(BF16)
        return carry

    lax.fori_loop(0, tm // rows, body, 0)


def _mod_spec(mod, k, d, tm, tiles_per_stream):
    if mod.ndim == 3:
        return pl.BlockSpec((None, 1, d), lambda i, j: (i // tiles_per_stream, 0, k))
    return pl.BlockSpec((tm, d), lambda i, j: (i, k))


def _row_tile(tm, n, mod, rows_per_stream):
    tm = min(tm, n if mod.ndim == 2 else rows_per_stream)
    return tm, max(rows_per_stream // tm, 1)


def _mm_in_kernel(x_ref, g_ref, sc_ref, sh_ref, w_ref, b_ref, wg_ref, bg_ref, z_ref, zg_ref, h_scr):
    @pl.when(pl.program_id(1) == 0)
    def _():
        _norm_mod_rows(x_ref, g_ref, sc_ref, sh_ref, h_scr)
        zg_ref[...] = _dot(h_scr[...], wg_ref[...]) + bg_ref[...]

    z_ref[...] = _dot(h_scr[...], w_ref[...]) + b_ref[...]


def _mm_in(x, g, mod, w, b, wg, bg, layer, nz, rows_per_stream, tm=1024, tn=1024):
    n, d = x.shape
    tm, tps = _row_tile(tm, n, mod, rows_per_stream)
    return pl.pallas_call(
        _mm_in_kernel,
        grid=(n // tm, nz // tn),
        in_specs=[pl.BlockSpec((tm, d), lambda i, j: (i, 0)),
                  pl.BlockSpec((None, 1, d), lambda i, j: (layer, 0, 0)),
                  _mod_spec(mod, 1, d, tm, tps),
                  _mod_spec(mod, 0, d, tm, tps),
                  pl.BlockSpec((None, d, tn), lambda i, j: (layer, 0, j)),
                  pl.BlockSpec((None, 1, tn), lambda i, j: (layer, 0, j)),
                  pl.BlockSpec((None, d, GATE_PAD), lambda i, j: (layer, 0, 0)),
                  pl.BlockSpec((None, 1, GATE_PAD), lambda i, j: (layer, 0, 0))],
        out_specs=[pl.BlockSpec((tm, tn), lambda i, j: (i, j)),
                   pl.BlockSpec((tm, GATE_PAD), lambda i, j: (i, 0))],
        out_shape=[jax.ShapeDtypeStruct((n, nz), F32),
                   jax.ShapeDtypeStruct((n, GATE_PAD), F32)],
        scratch_shapes=[pltpu.VMEM((tm, d), BF16)],
        compiler_params=_params("parallel", "arbitrary"),
        name="norm_in_proj",
    )(x, g, mod, mod, w, b, wg, bg)


def _mlstm_kernel(zq_ref, zk_ref, zv_ref, zo_ref, zg_ref, gt_ref, cw_ref, cb_ref, nrm_ref,
                  conv0_ref, c0_ref, n0_ref, m0_ref,
                  y_ref, c_ref, n_ref, m_ref, xq_scr, xk_scr):
    L = zq_ref.shape[0]
    hd = MA_HEAD_DIM
    heads = range(MA_HEADS)
    sls = [slice(h * hd, (h + 1) * hd) for h in heads]

    @pl.when(pl.program_id(1) == 0)
    def _():
        c_ref[...] = c0_ref[...]
        n_ref[...] = n0_ref[...]
        m_ref[...] = m0_ref[...]
        xq_scr[0:CONV_PAD, :] = conv0_ref[0, :, 0:MA_WIDTH]
        xk_scr[0:CONV_PAD, :] = conv0_ref[0, :, MA_WIDTH:2 * MA_WIDTH]

    xq_scr[CONV_PAD:CONV_PAD + L, :] = zq_ref[...]
    xk_scr[CONV_PAD:CONV_PAD + L, :] = zk_ref[...]

    def conv_silu(xs_ref, off):
        xs = xs_ref[...]
        acc = xs * cw_ref[0:1, off:off + MA_WIDTH]
        for j in range(1, CONV_W):
            acc = pltpu.roll(acc, 1, 0) + xs * cw_ref[j:j + 1, off:off + MA_WIDTH]
        acc = acc[CONV_PAD:, :] + cb_ref[:, off:off + MA_WIDTH]
        return acc * _sigmoid(acc)

    q = conv_silu(xq_scr, 0)
    k = conv_silu(xk_scr, MA_WIDTH) * (MA_HEAD_DIM ** -0.5)
    xq_scr[0:CONV_PAD, :] = xq_scr[L:L + CONV_PAD, :]
    xk_scr[0:CONV_PAD, :] = xk_scr[L:L + CONV_PAD, :]
    qb = q.astype(BF16)
    kb = k.astype(BF16)
    vb = zv_ref[...].astype(BF16)

    ti = lax.broadcasted_iota(jnp.int32, (L, L), 0)
    si = lax.broadcasted_iota(jnp.int32, (L, L), 1)
    causal = si <= ti
    gates_t = gt_ref[0, 0]

    d, w_inter, w_last, decay, m_t, m_last = [], [], [], [], [], []
    for h in heads:
        ig_col = zg_ref[:, h:h + 1]
        lf_col = _log_sigmoid(zg_ref[:, MA_HEADS + h:MA_HEADS + h + 1])
        ig_row = gates_t[h:h + 1, :]
        lf_row = _log_sigmoid(gates_t[MA_HEADS + h:MA_HEADS + h + 1, :])
        b_col = jnp.sum(jnp.where(causal, lf_row, 0.0), axis=1, keepdims=True)
        b_row = jnp.sum(jnp.where(ti <= si, lf_col, 0.0), axis=0, keepdims=True)
        log_d = jnp.where(causal, b_col - b_row + ig_row, NEG)
        log_inter = b_col + m_ref[0, h:h + 1, 0:1]
        m_h = jnp.maximum(log_inter, jnp.max(log_d, axis=1, keepdims=True))
        wi = jnp.exp(log_inter - m_h)
        ml = m_h[L - 1:L, :]
        d.append(jnp.exp(log_d - m_h))
        w_inter.append(wi)
        w_last.append(jnp.exp(b_col[L - 1:L, :] - b_col + ig_col - ml))
        decay.append(wi[L - 1:L, :])
        m_t.append(m_h)
        m_last.append(ml)

    s = [lax.dot_general(qb[:, sl], kb[:, sl], _NT, preferred_element_type=F32) * d[h]
         for h, sl in zip(heads, sls)]
    qc = [_dot(qb[:, sl], c_ref[0, h].astype(BF16)) for h, sl in zip(heads, sls)]
    sv = [_dot(s[h].astype(BF16), vb[:, sl]) for h, sl in zip(heads, sls)]
    kw = [k[:, sl] * w_last[h] for h, sl in zip(heads, sls)]
    upd = [lax.dot_general(kw[h].astype(BF16), vb[:, sl], _TN, preferred_element_type=F32)
           for h, sl in zip(heads, sls)]

    for h, sl in zip(heads, sls):
        n_row = n_ref[0, h:h + 1, :]
        qn = (jnp.sum(s[h], axis=1, keepdims=True)
              + w_inter[h] * jnp.sum(q[:, sl] * n_row, axis=1, keepdims=True))
        denom = jnp.maximum(jnp.abs(qn), jnp.exp(-m_t[h]))
        hh = (sv[h] + w_inter[h] * qc[h]) / denom
        y = hh * lax.rsqrt(jnp.mean(hh * hh, axis=-1, keepdims=True) + EPS)
        y = y * nrm_ref[:, sl] * _sigmoid(zo_ref[:, sl])
        y_ref[:, sl] = y.astype(BF16)
        c_ref[0, h] = decay[h] * c_ref[0, h] + upd[h]
        n_ref[0, h:h + 1, :] = decay[h] * n_row + jnp.sum(kw[h], axis=0, keepdims=True)
        m_ref[0, h:h + 1, :] = jnp.broadcast_to(m_last[h], (1, m_ref.shape[2]))


def _mlstm(z, zg, gates_t, conv_w, conv_b, ma_norm, conv0, c0, n0, m0, layer, batch, seq, L):
    nc = seq // L
    w = MA_WIDTH
    blk = lambda col: pl.BlockSpec((L, w), lambda b, c: (b * nc + c, col))
    st4 = pl.BlockSpec((1, MA_HEADS, MA_HEAD_DIM, MA_HEAD_DIM), lambda b, c: (b, 0, 0, 0))
    st3 = pl.BlockSpec((1, MA_HEADS, MA_HEAD_DIM), lambda b, c: (b, 0, 0))
    stm = pl.BlockSpec((1, MA_HEADS, GATE_PAD), lambda b, c: (b, 0, 0))
    return pl.pallas_call(
        _mlstm_kernel,
        grid=(batch, nc),
        in_specs=[blk(0), blk(1), blk(2), blk(3),
                  pl.BlockSpec((L, GATE_PAD), lambda b, c: (b * nc + c, 0)),
                  pl.BlockSpec((1, 1, 2 * MA_HEADS, L), lambda b, c: (b, c, 0, 0)),
                  pl.BlockSpec((None, CONV_W, 2 * w), lambda b, c: (layer, 0, 0)),
                  pl.BlockSpec((None, 1, 2 * w), lambda b, c: (layer, 0, 0)),
                  pl.BlockSpec((None, 1, w), lambda b, c: (layer, 0, 0)),
                  pl.BlockSpec((1, CONV_PAD, 2 * w), lambda b, c: (b, 0, 0)),
                  pl.BlockSpec((None, 1, MA_HEADS, MA_HEAD_DIM, MA_HEAD_DIM),
                               lambda b, c: (layer, b, 0, 0, 0)),
                  pl.BlockSpec((None, 1, MA_HEADS, MA_HEAD_DIM), lambda b, c: (layer, b, 0, 0)),
                  stm],
        out_specs=[pl.BlockSpec((L, w), lambda b, c: (b * nc + c, 0)), st4, st3, stm],
        out_shape=[jax.ShapeDtypeStruct((batch * seq, w), BF16),
                   jax.ShapeDtypeStruct(c0.shape[1:], F32),
                   jax.ShapeDtypeStruct(n0.shape[1:], F32),
                   jax.ShapeDtypeStruct(m0.shape, F32)],
        scratch_shapes=[pltpu.VMEM((CONV_PAD + L, w), F32), pltpu.VMEM((CONV_PAD + L, w), F32)],
        compiler_params=_params("parallel", "arbitrary"),
        name="mlstm_scan",
    )(z, z, z, z, zg, gates_t, conv_w, conv_b, ma_norm, conv0, c0, n0, m0)


def _hgrn_kernel(zq_ref, zf_ref, zi_ref, zog_ref, lb_ref, nrm_ref, s0_ref,
                 y_ref, s_ref, st_scr, b_scr, k_scr, q_scr, o_scr):
    L = zq_ref.shape[0]
    hd = HB_DIM
    heads = range(HB_HEADS)
    sls = [slice(h * hd, (h + 1) * hd) for h in heads]
    nlast = pl.num_programs(1) - 1

    @pl.when(pl.program_id(1) == 0)
    def _():
        for h in heads:
            st_scr[h] = s0_ref[0, h].T

    lb = lb_ref[...]
    fb = zf_ref[...]
    log_f = jnp.log(lb + (1.0 - lb) * _sigmoid(fb))
    k = (1.0 - lb) * _sigmoid(-fb)
    qr = zq_ref[...]
    q = qr * _sigmoid(qr)

    ti = lax.broadcasted_iota(jnp.int32, (L, L), 0)
    si = lax.broadcasted_iota(jnp.int32, (L, L), 1)
    causal = si <= ti
    tril = causal.astype(BF16)
    f_hi = log_f.astype(BF16)
    r1 = log_f - f_hi.astype(F32)
    f_mid = r1.astype(BF16)
    f_lo = (r1 - f_mid.astype(F32)).astype(BF16)
    b = _dot(tril, f_hi) + _dot(tril, f_mid) + _dot(tril, f_lo)

    mid = L // 2
    r = b[mid - 1:mid, :]
    b_last = b[L - 1:L, :]
    e_last = jnp.exp(b_last)
    qs = (q * jnp.exp(b)).astype(BF16)
    ks = (k * jnp.exp(b_last - b)).astype(BF16)
    vb = zi_ref[...].astype(BF16)

    inter = [lax.dot_general(qs[:, sl], st_scr[h].astype(BF16), _NT, preferred_element_type=F32)
             for h, sl in zip(heads, sls)]
    upd = [lax.dot_general(vb[:, sl], ks[:, sl], _TN, preferred_element_type=F32) for sl in sls]
    for h, sl in zip(heads, sls):
        o_scr[:, sl] = inter[h]
        st_scr[h] = e_last[:, sl] * st_scr[h] + upd[h]

    spread = jnp.max(jnp.maximum(-r, r - b_last))
    safe = spread <= HGRN_SAFE_EXP

    @pl.when(safe)
    def _():
        qt = (q * jnp.exp(b - r)).astype(BF16)
        kt = (k * jnp.exp(r - b)).astype(BF16)
        a_all = [lax.dot_general(qt[:, sl], kt[:, sl], _NT, preferred_element_type=F32) for sl in sls]
        a_all = [jnp.where(causal, a, 0.0).astype(BF16) for a in a_all]
        intra = [_dot(a, vb[:, sl]) for a, sl in zip(a_all, sls)]
        for sl, o in zip(sls, intra):
            o_scr[:, sl] += o

    @pl.when(jnp.logical_not(safe))
    def _():
        b_scr[...] = b
        k_scr[...] = k
        q_scr[...] = q
        width = b.shape[1]
        head_of_lane = lax.broadcasted_iota(jnp.int32, (width, GATE_PAD), 0) // hd
        col = lax.broadcasted_iota(jnp.int32, (width, GATE_PAD), 1)
        gather = (head_of_lane == col).astype(BF16)
        head_of_lane_t = lax.broadcasted_iota(jnp.int32, (GATE_PAD, width), 1) // hd
        row = lax.broadcasted_iota(jnp.int32, (GATE_PAD, width), 0)
        scatter = (head_of_lane_t == row).astype(BF16)
        t_idx = lax.broadcasted_iota(jnp.int32, (L, GATE_PAD), 0)

        def body(s, carry):
            b_s = b_scr[pl.ds(s, 1), :]
            k_s = k_scr[pl.ds(s, 1), :]
            v_s = zi_ref[pl.ds(s, 1), :]
            e = jnp.exp(jnp.minimum(b_scr[...] - b_s, 0.0)) * q_scr[...] * k_s
            a = jnp.where(t_idx >= s, _dot(e.astype(BF16), gather), 0.0)
            o_scr[...] += _dot(a.astype(BF16), scatter) * v_s
            return carry

        lax.fori_loop(0, L, body, 0)

    o_all = [o_scr[:, sl] for sl in sls]
    inv = [lax.rsqrt(jnp.mean(o * o, axis=-1, keepdims=True) + EPS) for o in o_all]
    for sl, o, iv in zip(sls, o_all, inv):
        y_ref[:, sl] = (o * iv * nrm_ref[:, sl] * _sigmoid(zog_ref[:, sl])).astype(BF16)

    @pl.when(pl.program_id(1) == nlast)
    def _():
        for h in heads:
            s_ref[0, h] = st_scr[h].T


def _hgrn(z, lb_all, hb_norm, s0, layer, batch, seq, L):
    nc = seq // L
    w = HB_WIDTH
    blk = lambda col: pl.BlockSpec((L, w), lambda b, c: (b * nc + c, col))
    st = pl.BlockSpec((1, HB_HEADS, HB_DIM, HB_DIM), lambda b, c: (b, 0, 0, 0))
    return pl.pallas_call(
        _hgrn_kernel,
        grid=(batch, nc),
        in_specs=[blk(4), blk(5), blk(6), blk(7),
                  pl.BlockSpec((None, 1, w), lambda b, c: (layer, 0, 0)),
                  pl.BlockSpec((None, 1, w), lambda b, c: (layer, 0, 0)),
                  pl.BlockSpec((None, 1, HB_HEADS, HB_DIM, HB_DIM), lambda b, c: (layer, b, 0, 0, 0))],
        out_specs=[pl.BlockSpec((L, w), lambda b, c: (b * nc + c, 0)), st],
        out_shape=[jax.ShapeDtypeStruct((batch * seq, w), BF16),
                   jax.ShapeDtypeStruct(s0.shape[1:], F32)],
        scratch_shapes=[pltpu.VMEM((HB_HEADS, HB_DIM, HB_DIM), F32),
                        pltpu.VMEM((L, w), F32), pltpu.VMEM((L, w), F32),
                        pltpu.VMEM((L, w), F32), pltpu.VMEM((L, w), F32)],
        compiler_params=_params("parallel", "arbitrary"),
        name="hgrn_scan",
    )(z, z, z, z, lb_all, hb_norm, s0)


def _mm_br_kernel(ya_ref, yb_ref, wa_ref, wb_ref, ga_ref, gb_ref, o_ref):
    ta = _dot(ya_ref[...], wa_ref[...])
    tb = _dot(yb_ref[...], wb_ref[...])
    o_ref[...] = (_sigmoid(ga_ref[...]) * ta + _sigmoid(gb_ref[...]) * tb).astype(BF16)


def _mm_branches(ya, yb, wa, wb, z, ga_col, gb_col, layer, tm=1024, tn=512):
    n, kdim = ya.shape
    d = wa.shape[2]
    tm = min(tm, n)
    ga_blk, gb_blk = ga_col // tn, gb_col // tn
    return pl.pallas_call(
        _mm_br_kernel,
        grid=(n // tm, d // tn),
        in_specs=[pl.BlockSpec((tm, kdim), lambda i, j: (i, 0)),
                  pl.BlockSpec((tm, kdim), lambda i, j: (i, 0)),
                  pl.BlockSpec((None, kdim, tn), lambda i, j: (layer, 0, j)),
                  pl.BlockSpec((None, kdim, tn), lambda i, j: (layer, 0, j)),
                  pl.BlockSpec((tm, tn), lambda i, j: (i, ga_blk + j)),
                  pl.BlockSpec((tm, tn), lambda i, j: (i, gb_blk + j))],
        out_specs=pl.BlockSpec((tm, tn), lambda i, j: (i, j)),
        out_shape=jax.ShapeDtypeStruct((n, d), BF16),
        compiler_params=_params("parallel", "arbitrary"),
        name="branch_merge",
    )(ya, yb, wa, wb, z, z)


def _mm_res_kernel(a_ref, w_ref, x_ref, gt_ref, o_ref):
    o_ref[...] = x_ref[...] + gt_ref[...] * _dot(a_ref[...], w_ref[...])


def _mm_residual(a, w, x, mod, gate_chunk, layer, rows_per_stream, tm=1024, tn=512):
    n, kdim = a.shape
    d = w.shape[2]
    tm, tps = _row_tile(tm, n, mod, rows_per_stream)
    nj = d // tn
    if mod.ndim == 3:
        gate_spec = pl.BlockSpec((None, 1, tn), lambda i, j: (i // tps, 0, gate_chunk * nj + j))
    else:
        gate_spec = pl.BlockSpec((tm, tn), lambda i, j: (i, gate_chunk * nj + j))
    return pl.pallas_call(
        _mm_res_kernel,
        grid=(n // tm, nj),
        in_specs=[pl.BlockSpec((tm, kdim), lambda i, j: (i, 0)),
                  pl.BlockSpec((None, kdim, tn), lambda i, j: (layer, 0, j)),
                  pl.BlockSpec((tm, tn), lambda i, j: (i, j)),
                  gate_spec],
        out_specs=pl.BlockSpec((tm, tn), lambda i, j: (i, j)),
        out_shape=jax.ShapeDtypeStruct((n, d), F32),
        compiler_params=_params("parallel", "arbitrary"),
        name="out_proj_residual",
    )(a, w, x, mod)


def _ffn_kernel(x_ref, g_ref, sc_ref, sh_ref, gt_ref, wu_ref, wd_ref, fg_ref, o_ref, h_scr, *, final):
    f = pl.program_id(1)

    @pl.when(f == 0)
    def _():
        _norm_mod_rows(x_ref, g_ref, sc_ref, sh_ref, h_scr)
        o_ref[...] = jnp.zeros_like(o_ref)

    u = jnp.square(jnp.maximum(_dot(h_scr[...], wu_ref[...]), 0.0))
    o_ref[...] += _dot(u.astype(BF16), wd_ref[...])

    @pl.when(f == pl.num_programs(1) - 1)
    def _():
        tm = x_ref.shape[0]
        rows = min(tm, 256)

        def body(r, carry):
            sl = pl.ds(pl.multiple_of(r * rows, rows), rows)
            gt = gt_ref[...] if gt_ref.shape[0] == 1 else gt_ref[sl, :]
            xn = x_ref[sl, :] + gt * o_ref[sl, :]
            if final:
                xn = xn * lax.rsqrt(jnp.mean(xn * xn, axis=-1, keepdims=True) + EPS) * fg_ref[...]
            o_ref[sl, :] = xn
            return carry

        lax.fori_loop(0, tm // rows, body, 0)


def _ffn(x, g, mod, wu, wd, final_g, final, layer, rows_per_stream, tm=512, tf=512):
    n, d = x.shape
    dff = wu.shape[2]
    tm, tps = _row_tile(tm, n, mod, rows_per_stream)
    return pl.pallas_call(
        functools.partial(_ffn_kernel, final=final),
        grid=(n // tm, dff // tf),
        in_specs=[pl.BlockSpec((tm, d), lambda i, j: (i, 0)),
                  pl.BlockSpec((None, 1, d), lambda i, j: (layer, 0, 0)),
                  _mod_spec(mod, 4, d, tm, tps),
                  _mod_spec(mod, 3, d, tm, tps),
                  _mod_spec(mod, 5, d, tm, tps),
                  pl.BlockSpec((None, d, tf), lambda i, j: (layer, 0, j)),
                  pl.BlockSpec((None, tf, d), lambda i, j: (layer, j, 0)),
                  pl.BlockSpec((1, d), lambda i, j: (0, 0))],
        out_specs=pl.BlockSpec((tm, d), lambda i, j: (i, 0)),
        out_shape=jax.ShapeDtypeStruct((n, d), F32),
        scratch_shapes=[pltpu.VMEM((tm, d), BF16)],
        compiler_params=_params("parallel", "arbitrary"),
        name="ffn_residual",
    )(x, g, mod, mod, mod, wu, wd, final_g)


def _trunk(x, mod_all, per_token_mod, conv_c, st_c, st_n, st_m, st_s, lb_all, wts):
    batch, seq, d = x.shape
    depth = conv_c.shape[0]
    L = min(MA_CHUNK, seq)
    Lh = min(HB_CHUNK, seq)
    nc = seq // L
    n = batch * seq
    nz = wts["b_in"].shape[2]
    keep = CONV_W - 1
    xf = x.reshape(n, d)
    bufs, cms, nvs, mrs, sms = [], [], [], [], []
    for l in range(depth):
        mod = mod_all[l]
        if per_token_mod:
            mod = jnp.repeat(mod, seq, axis=0)
        else:
            mod = mod[:, None, :]
        z, zg = _mm_in(xf, wts["norm1_g"], mod, wts["w_in"], wts["b_in"], wts["w_gate"], wts["b_gate"],
                       l, nz, seq)
        gates_t = jnp.swapaxes(zg[:, :2 * MA_HEADS].reshape(batch, nc, L, 2 * MA_HEADS), 2, 3)
        conv0 = jnp.pad(conv_c[l], ((0, 0), (CONV_PAD - keep, 0), (0, 0)))
        m0 = jnp.broadcast_to(st_m[l][:, :, None], (batch, MA_HEADS, GATE_PAD))
        ya, c1, n1, m1 = _mlstm(z, zg, gates_t, wts["conv_w"], wts["conv_b"], wts["ma_norm"],
                                conv0, st_c, st_n, m0, l, batch, seq, L)
        yb, s1 = _hgrn(z, lb_all, wts["hb_norm"], st_s, l, batch, seq, Lh)
        merged = _mm_branches(ya, yb, wts["w_br_a"], wts["w_br_b"], z,
                              4 * MA_WIDTH + 4 * HB_WIDTH, 4 * MA_WIDTH + 4 * HB_WIDTH + d, l)
        xf = _mm_residual(merged, wts["w_o"], xf, mod, 2, l, seq)
        xf = _ffn(xf, wts["norm2_g"], mod, wts["w_up"], wts["w_down"], wts["final_g"],
                  l == depth - 1, l, seq)
        bufs.append(z.reshape(batch, seq, nz)[:, seq - keep:, :2 * MA_WIDTH])
        cms.append(c1)
        nvs.append(n1)
        mrs.append(m1[:, :, 0])
        sms.append(s1)
    return (xf.reshape(batch, seq, d), jnp.stack(bufs), jnp.stack(cms), jnp.stack(nvs),
            jnp.stack(mrs), jnp.stack(sms))


def kernel(x_prompt, x_sample, cache_conv, state_mlstm_C, state_mlstm_n, state_mlstm_m, state_hgrn,
           c_prompt, c_sample, ada_w, ada_b, norm1_g, norm2_g, w_in, b_in, conv_w, conv_b, ma_norm,
           hgrn_lb_raw, hb_norm, w_br_a, w_br_b, w_o, w_up, w_down, final_g):
    depth, d, n_in = w_in.shape
    bp, bs = x_prompt.shape[0], x_sample.shape[0]
    n_main = n_in - 2 * MA_HEADS
    gate_pad = GATE_PAD - 2 * MA_HEADS

    lb_all = _lower_bounds(hgrn_lb_raw).reshape(depth, 1, HB_WIDTH)
    c_all = jnp.concatenate([c_prompt, c_sample], axis=0)
    c_rows = -(-c_all.shape[0] // 8) * 8
    c_all = jnp.pad(c_all, ((0, c_rows - c_all.shape[0]), (0, 0)))
    mod_all = _modulation(c_all, ada_w, ada_b)

    wts = {
        "norm1_g": norm1_g.reshape(depth, 1, d),
        "norm2_g": norm2_g.reshape(depth, 1, d),
        "w_in": w_in.astype(BF16),
        "b_in": b_in[:, :n_main].reshape(depth, 1, n_main),
        "w_gate": jnp.pad(w_in[:, :, n_main:], ((0, 0), (0, 0), (0, gate_pad))).astype(BF16),
        "b_gate": jnp.pad(b_in[:, n_main:], ((0, 0), (0, gate_pad))).reshape(depth, 1, GATE_PAD),
        "conv_w": conv_w,
        "conv_b": conv_b.reshape(depth, 1, -1),
        "ma_norm": ma_norm.reshape(depth, 1, -1),
        "hb_norm": hb_norm.reshape(depth, 1, -1),
        "w_br_a": w_br_a.astype(BF16),
        "w_br_b": w_br_b.astype(BF16),
        "w_o": w_o.astype(BF16),
        "w_up": w_up.astype(BF16),
        "w_down": w_down.astype(BF16),
        "final_g": final_g.reshape(1, d),
    }

    f32 = jnp.float32
    z_conv = jnp.zeros((depth, bp, CONV_W - 1, 2 * MA_WIDTH), f32)
    z_c = jnp.zeros((depth, bp, MA_HEADS, MA_HEAD_DIM, MA_HEAD_DIM), f32)
    z_n = jnp.zeros((depth, bp, MA_HEADS, MA_HEAD_DIM), f32)
    z_m = jnp.zeros((depth, bp, MA_HEADS), f32)
    z_s = jnp.zeros((depth, bp, HB_HEADS, HB_DIM, HB_DIM), f32)

    out_p = _trunk(x_prompt, mod_all[:, :bp], False, z_conv, z_c, z_n, z_m, z_s, lb_all, wts)
    out_s = _trunk(x_sample, mod_all[:, bp:bp + bs], True, cache_conv, state_mlstm_C, state_mlstm_n,
                   state_mlstm_m, state_hgrn, lb_all, wts)
    y_p, conv_p, c_p, n_p, m_p, s_p = out_p
    y_s, conv_s, c_s, n_s, m_s, s_s = out_s
    return (y_p, y_s, conv_p, c_p, n_p, m_p, s_p, conv_s, c_s, n_s, m_s, s_s)
```

```python
import functools

import jax
import jax.numpy as jnp
from jax import lax
from jax.experimental import pallas as pl
from jax.experimental.pallas import tpu as pltpu

EPS = 1e-6
NEG = -1e30
MA_CHUNK = 256
HB_CHUNK = 64
MA_HEADS = 8
MA_HEAD_DIM = 256
MA_WIDTH = MA_HEADS * MA_HEAD_DIM
CONV_W = 4
HB_HEADS = 16
HB_DIM = 128
HB_WIDTH = HB_HEADS * HB_DIM
GATE_PAD = 128
CONV_PAD = 8
HGRN_SAFE_EXP = 80.0
VMEM_LIMIT = 56 * 1024 * 1024

F32 = jnp.float32
BF16 = jnp.bfloat16

_NT = (((1,), (1,)), ((), ()))
_TN = (((0,), (0,)), ((), ()))


def _params(*sem):
    return pltpu.CompilerParams(dimension_semantics=sem, vmem_limit_bytes=VMEM_LIMIT)


def _sigmoid(x):
    return 1.0 / (1.0 + jnp.exp(-x))


def _log_sigmoid(x):
    return jnp.minimum(x, 0.0) - jnp.log1p(jnp.exp(-jnp.abs(x)))


def _dot(a, b):
    return jnp.dot(a, b, preferred_element_type=F32)


def _lb_kernel(raw_ref, o_ref):
    x = raw_ref[...]
    e = jnp.exp(x - jnp.max(x, axis=0, keepdims=True))
    sm = e / jnp.sum(e, axis=0, keepdims=True)
    depth = x.shape[0]
    acc = sm[0:1, :]
    o_ref[0:1, :] = acc - sm[0:1, :]
    for l in range(1, depth):
        acc = acc + sm[l:l + 1, :]
        o_ref[l:l + 1, :] = acc - sm[0:1, :]


def _lower_bounds(raw):
    return pl.pallas_call(
        _lb_kernel, out_shape=jax.ShapeDtypeStruct(raw.shape, F32), name="hgrn_lower_bounds")(raw)


def _cast_kernel(w_ref, o_ref):
    o_ref[...] = w_ref[...].astype(BF16)


def _to_bf16(w, rows):
    depth, r, c = w.shape
    spec = pl.BlockSpec((None, rows, c), lambda l, i: (l, i, 0))
    return pl.pallas_call(
        _cast_kernel, grid=(depth, r // rows), in_specs=[spec], out_specs=spec,
        out_shape=jax.ShapeDtypeStruct(w.shape, BF16),
        compiler_params=_params("parallel", "parallel"), name="weights_to_bf16")(w)


def _ada_kernel(c_ref, w_ref, b_ref, o_ref):
    c = c_ref[...]
    cs = (c * _sigmoid(c)).astype(BF16)
    o_ref[...] = _dot(cs, w_ref[...].astype(BF16)) + b_ref[...]


def _modulation(c_all, ada_w, ada_b, tn=1024):
    depth, d, n6 = ada_w.shape
    rows = c_all.shape[0]
    return pl.pallas_call(
        _ada_kernel,
        grid=(depth, n6 // tn),
        in_specs=[pl.BlockSpec((rows, d), lambda l, j: (0, 0)),
                  pl.BlockSpec((None, d, tn), lambda l, j: (l, 0, j)),
                  pl.BlockSpec((None, 1, tn), lambda l, j: (l, 0, j))],
        out_specs=pl.BlockSpec((None, rows, tn), lambda l, j: (l, 0, j)),
        out_shape=jax.ShapeDtypeStruct((depth, rows, n6), F32),
        compiler_params=_params("parallel", "parallel"),
        name="ada_modulation",
    )(c_all, ada_w, ada_b.reshape(depth, 1, n6))


def _norm_mod_rows(x_ref, g_ref, sc_ref, sh_ref, h_scr):
    tm = x_ref.shape[0]
    rows = min(tm, 256)

    def body(r, carry):
        sl = pl.ds(pl.multiple_of(r * rows, rows), rows)
        x = x_ref[sl, :]
        y = x * lax.rsqrt(jnp.mean(x * x, axis=-1, keepdims=True) + EPS) * g_ref[...]
        sc = sc_ref[...] if sc_ref.shape[0] == 1 else sc_ref[sl, :]
        sh = sh_ref[...] if sh_ref.shape[0] == 1 else sh_ref[sl, :]
        h_scr[sl, :] = (y * (1.0 + sc) + sh).astype(BF16)
        return carry

    lax.fori_loop(0, tm // rows, body, 0)


def _mod_spec(mod, k, d, tm, tiles_per_stream):
    if mod.ndim == 3:
        return pl.BlockSpec((None, 1, d), lambda i, j: (i // tiles_per_stream, 0, k))
    return pl.BlockSpec((tm, d), lambda i, j: (i, k))


def _row_tile(tm, n, mod, rows_per_stream):
    tm = min(tm, n if mod.ndim == 2 else rows_per_stream)
    return tm, max(rows_per_stream // tm, 1)


def _norm_kernel(x_ref, g_ref, sc_ref, sh_ref, wg_ref, bg_ref, h_ref, zg_ref):
    _norm_mod_rows(x_ref, g_ref, sc_ref, sh_ref, h_ref)
    zg_ref[...] = _dot(h_ref[...], wg_ref[...]) + bg_ref[...]


def _norm_gates(x, g, mod, wg, bg, layer, rows_per_stream, tm=512):
    n, d = x.shape
    tm, tps = _row_tile(tm, n, mod, rows_per_stream)
    return pl.pallas_call(
        _norm_kernel,
        grid=(n // tm, 1),
        in_specs=[pl.BlockSpec((tm, d), lambda i, j: (i, 0)),
                  pl.BlockSpec((None, 1, d), lambda i, j: (layer, 0, 0)),
                  _mod_spec(mod, 1, d, tm, tps),
                  _mod_spec(mod, 0, d, tm, tps),
                  pl.BlockSpec((None, d, GATE_PAD), lambda i, j: (layer, 0, 0)),
                  pl.BlockSpec((None, 1, GATE_PAD), lambda i, j: (layer, 0, 0))],
        out_specs=[pl.BlockSpec((tm, d), lambda i, j: (i, 0)),
                   pl.BlockSpec((tm, GATE_PAD), lambda i, j: (i, 0))],
        out_shape=[jax.ShapeDtypeStruct((n, d), BF16),
                   jax.ShapeDtypeStruct((n, GATE_PAD), F32)],
        compiler_params=_params("parallel", "arbitrary"),
        name="norm_gates",
    )(x, g, mod, mod, wg, bg)


def _mm_in_kernel(h_ref, w_ref, b_ref, z_ref):
    z_ref[...] = _dot(h_ref[...], w_ref[...]) + b_ref[...]


def _mm_in(h, w, b, layer, nz, tm=2048, tn=1024):
    n, d = h.shape
    tm = min(tm, n)
    return pl.pallas_call(
        _mm_in_kernel,
        grid=(n // tm, nz // tn),
        in_specs=[pl.BlockSpec((tm, d), lambda i, j: (i, 0)),
                  pl.BlockSpec((None, d, tn), lambda i, j: (layer, 0, j)),
                  pl.BlockSpec((None, 1, tn), lambda i, j: (layer, 0, j))],
        out_specs=pl.BlockSpec((tm, tn), lambda i, j: (i, j)),
        out_shape=jax.ShapeDtypeStruct((n, nz), F32),
        compiler_params=_params("parallel", "arbitrary"),
        name="in_proj",
    )(h, w, b)


def _mlstm_kernel(zq_ref, zk_ref, zv_ref, zo_ref, zg_ref, gt_ref, cw_ref, cb_ref, nrm_ref,
                  conv0_ref, c0_ref, n0_ref, m0_ref,
                  y_ref, c_ref, n_ref, m_ref, xq_scr, xk_scr):
    L = zq_ref.shape[0]
    hd = MA_HEAD_DIM
    heads = range(MA_HEADS)
    sls = [slice(h * hd, (h + 1) * hd) for h in heads]

    @pl.when(pl.program_id(1) == 0)
    def _():
        c_ref[...] = c0_ref[...]
        n_ref[...] = n0_ref[...]
        m_ref[...] = m0_ref[...]
        xq_scr[0:CONV_PAD, :] = conv0_ref[0, :, 0:MA_WIDTH]
        xk_scr[0:CONV_PAD, :] = conv0_ref[0, :, MA_WIDTH:2 * MA_WIDTH]

    xq_scr[CONV_PAD:CONV_PAD + L, :] = zq_ref[...]
    xk_scr[CONV_PAD:CONV_PAD + L, :] = zk_ref[...]

    def conv_silu(xs_ref, off):
        xs = xs_ref[...]
        acc = xs * cw_ref[0:1, off:off + MA_WIDTH]
        for j in range(1, CONV_W):
            acc = pltpu.roll(acc, 1, 0) + xs * cw_ref[j:j + 1, off:off + MA_WIDTH]
        acc = acc[CONV_PAD:, :] + cb_ref[:, off:off + MA_WIDTH]
        return acc * _sigmoid(acc)

    q = conv_silu(xq_scr, 0)
    k = conv_silu(xk_scr, MA_WIDTH) * (MA_HEAD_DIM ** -0.5)
    xq_scr[0:CONV_PAD, :] = xq_scr[L:L + CONV_PAD, :]
    xk_scr[0:CONV_PAD, :] = xk_scr[L:L + CONV_PAD, :]
    qb = q.astype(BF16)
    kb = k.astype(BF16)
    vb = zv_ref[...].astype(BF16)

    ti = lax.broadcasted_iota(jnp.int32, (L, L), 0)
    si = lax.broadcasted_iota(jnp.int32, (L, L), 1)
    causal = si <= ti
    gates_t = gt_ref[0, 0]

    d, w_inter, w_last, decay, m_t, m_last = [], [], [], [], [], []
    for h in heads:
        ig_col = zg_ref[:, h:h + 1]
        lf_col = _log_sigmoid(zg_ref[:, MA_HEADS + h:MA_HEADS + h + 1])
        ig_row = gates_t[h:h + 1, :]
        lf_row = _log_sigmoid(gates_t[MA_HEADS + h:MA_HEADS + h + 1, :])
        b_col = jnp.sum(jnp.where(causal, lf_row, 0.0), axis=1, keepdims=True)
        b_row = jnp.sum(jnp.where(ti <= si, lf_col, 0.0), axis=0, keepdims=True)
        log_d = jnp.where(causal, b_col - b_row + ig_row, NEG)
        log_inter = b_col + m_ref[0, h:h + 1, 0:1]
        m_h = jnp.maximum(log_inter, jnp.max(log_d, axis=1, keepdims=True))
        wi = jnp.exp(log_inter - m_h)
        ml = m_h[L - 1:L, :]
        d.append(jnp.exp(log_d - m_h))
        w_inter.append(wi)
        w_last.append(jnp.exp(b_col[L - 1:L, :] - b_col + ig_col - ml))
        decay.append(wi[L - 1:L, :])
        m_t.append(m_h)
        m_last.append(ml)

    s = [lax.dot_general(qb[:, sl], kb[:, sl], _NT, preferred_element_type=F32) * d[h]
         for h, sl in zip(heads, sls)]
    qc = [_dot(qb[:, sl], c_ref[0, h].astype(BF16)) for h, sl in zip(heads, sls)]
    sv = [_dot(s[h].astype(BF16), vb[:, sl]) for h, sl in zip(heads, sls)]
    kw = [k[:, sl] * w_last[h] for h, sl in zip(heads, sls)]
    upd = [lax.dot_general(kw[h].astype(BF16), vb[:, sl], _TN, preferred_element_type=F32)
           for h, sl in zip(heads, sls)]

    for h, sl in zip(heads, sls):
        n_row = n_ref[0, h:h + 1, :]
        qn = (jnp.sum(s[h], axis=1, keepdims=True)
              + w_inter[h] * jnp.sum(q[:, sl] * n_row, axis=1, keepdims=True))
        denom = jnp.maximum(jnp.abs(qn), jnp.exp(-m_t[h]))
        hh = (sv[h] + w_inter[h] * qc[h]) / denom
        y = hh * lax.rsqrt(jnp.mean(hh * hh, axis=-1, keepdims=True) + EPS)
        y = y * nrm_ref[:, sl] * _sigmoid(zo_ref[:, sl])
        y_ref[:, sl] = y.astype(BF16)
        c_ref[0, h] = decay[h] * c_ref[0, h] + upd[h]
        n_ref[0, h:h + 1, :] = decay[h] * n_row + jnp.sum(kw[h], axis=0, keepdims=True)
        m_ref[0, h:h + 1, :] = jnp.broadcast_to(m_last[h], (1, m_ref.shape[2]))


def _mlstm(z, zg, gates_t, conv_w, conv_b, ma_norm, conv0, c0, n0, m0, layer, batch, seq, L):
    nc = seq // L
    w = MA_WIDTH
    blk = lambda col: pl.BlockSpec((L, w), lambda b, c: (b * nc + c, col))
    st4 = pl.BlockSpec((1, MA_HEADS, MA_HEAD_DIM, MA_HEAD_DIM), lambda b, c: (b, 0, 0, 0))
    st3 = pl.BlockSpec((1, MA_HEADS, MA_HEAD_DIM), lambda b, c: (b, 0, 0))
    stm = pl.BlockSpec((1, MA_HEADS, GATE_PAD), lambda b, c: (b, 0, 0))
    return pl.pallas_call(
        _mlstm_kernel,
        grid=(batch, nc),
        in_specs=[blk(0), blk(1), blk(2), blk(3),
                  pl.BlockSpec((L, GATE_PAD), lambda b, c: (b * nc + c, 0)),
                  pl.BlockSpec((1, 1, 2 * MA_HEADS, L), lambda b, c: (b, c, 0, 0)),
                  pl.BlockSpec((None, CONV_W, 2 * w), lambda b, c: (layer, 0, 0)),
                  pl.BlockSpec((None, 1, 2 * w), lambda b, c: (layer, 0, 0)),
                  pl.BlockSpec((None, 1, w), lambda b, c: (layer, 0, 0)),
                  pl.BlockSpec((1, CONV_PAD, 2 * w), lambda b, c: (b, 0, 0)),
                  pl.BlockSpec((None, 1, MA_HEADS, MA_HEAD_DIM, MA_HEAD_DIM),
                               lambda b, c: (layer, b, 0, 0, 0)),
                  pl.BlockSpec((None, 1, MA_HEADS, MA_HEAD_DIM), lambda b, c: (layer, b, 0, 0)),
                  stm],
        out_specs=[pl.BlockSpec((L, w), lambda b, c: (b * nc + c, 0)), st4, st3, stm],
        out_shape=[jax.ShapeDtypeStruct((batch * seq, w), BF16),
                   jax.ShapeDtypeStruct(c0.shape[1:], F32),
                   jax.ShapeDtypeStruct(n0.shape[1:], F32),
                   jax.ShapeDtypeStruct(m0.shape, F32)],
        scratch_shapes=[pltpu.VMEM((CONV_PAD + L, w), F32), pltpu.VMEM((CONV_PAD + L, w), F32)],
        compiler_params=_params("parallel", "arbitrary"),
        name="mlstm_scan",
    )(z, z, z, z, zg, gates_t, conv_w, conv_b, ma_norm, conv0, c0, n0, m0)


def _hgrn_kernel(zq_ref, zf_ref, zi_ref, zog_ref, lb_ref, nrm_ref, s0_ref,
                 y_ref, s_ref, st_scr, b_scr, k_scr, q_scr, o_scr):
    L = zq_ref.shape[0]
    hd = HB_DIM
    heads = range(HB_HEADS)
    sls = [slice(h * hd, (h + 1) * hd) for h in heads]
    nlast = pl.num_programs(1) - 1

    @pl.when(pl.program_id(1) == 0)
    def _():
        for h in heads:
            st_scr[h] = s0_ref[0, h].T

    lb = lb_ref[...]
    fb = zf_ref[...]
    log_f = jnp.log(lb + (1.0 - lb) * _sigmoid(fb))
    k = (1.0 - lb) * _sigmoid(-fb)
    qr = zq_ref[...]
    q = qr * _sigmoid(qr)

    ti = lax.broadcasted_iota(jnp.int32, (L, L), 0)
    si = lax.broadcasted_iota(jnp.int32, (L, L), 1)
    causal = si <= ti
    tril = causal.astype(BF16)
    f_hi = log_f.astype(BF16)
    r1 = log_f - f_hi.astype(F32)
    f_mid = r1.astype(BF16)
    f_lo = (r1 - f_mid.astype(F32)).astype(BF16)
    b = _dot(tril, f_hi) + _dot(tril, f_mid) + _dot(tril, f_lo)

    mid = L // 2
    r = b[mid - 1:mid, :]
    b_last = b[L - 1:L, :]
    e_last = jnp.exp(b_last)
    qs = (q * jnp.exp(b)).astype(BF16)
    ks = (k * jnp.exp(b_last - b)).astype(BF16)
    vb = zi_ref[...].astype(BF16)

    inter = [lax.dot_general(qs[:, sl], st_scr[h].astype(BF16), _NT, preferred_element_type=F32)
             for h, sl in zip(heads, sls)]
    upd = [lax.dot_general(vb[:, sl], ks[:, sl], _TN, preferred_element_type=F32) for sl in sls]
    for h, sl in zip(heads, sls):
        o_scr[:, sl] = inter[h]
        st_scr[h] = e_last[:, sl] * st_scr[h] + upd[h]

    spread = jnp.max(jnp.maximum(-r, r - b_last))
    safe = spread <= HGRN_SAFE_EXP

    @pl.when(safe)
    def _():
        qt = (q * jnp.exp(b - r)).astype(BF16)
        kt = (k * jnp.exp(r - b)).astype(BF16)
        a_all = [lax.dot_general(qt[:, sl], kt[:, sl], _NT, preferred_element_type=F32) for sl in sls]
        a_all = [jnp.where(causal, a, 0.0).astype(BF16) for a in a_all]
        intra = [_dot(a, vb[:, sl]) for a, sl in zip(a_all, sls)]
        for sl, o in zip(sls, intra):
            o_scr[:, sl] += o

    @pl.when(jnp.logical_not(safe))
    def _():
        b_scr[...] = b
        k_scr[...] = k
        q_scr[...] = q
        width = b.shape[1]
        head_of_lane = lax.broadcasted_iota(jnp.int32, (width, GATE_PAD), 0) // hd
        col = lax.broadcasted_iota(jnp.int32, (width, GATE_PAD), 1)
        gather = (head_of_lane == col).astype(BF16)
        head_of_lane_t = lax.broadcasted_iota(jnp.int32, (GATE_PAD, width), 1) // hd
        row = lax.broadcasted_iota(jnp.int32, (GATE_PAD, width), 0)
        scatter = (head_of_lane_t == row).astype(BF16)
        t_idx = lax.broadcasted_iota(jnp.int32, (L, GATE_PAD), 0)

        def body(s, carry):
            b_s = b_scr[pl.ds(s, 1), :]
            k_s = k_scr[pl.ds(s, 1), :]
            v_s = zi_ref[pl.ds(s, 1), :]
            e = jnp.exp(jnp.minimum(b_scr[...] - b_s, 0.0)) * q_scr[...] * k_s
            a = jnp.where(t_idx >= s, _dot(e.astype(BF16), gather), 0.0)
            o_scr[...] += _dot(a.astype(BF16), scatter) * v_s
            return carry

        lax.fori_loop(0, L, body, 0)

    o_all = [o_scr[:, sl] for sl in sls]
    inv = [lax.rsqrt(jnp.mean(o * o, axis=-1, keepdims=True) + EPS) for o in o_all]
    for sl, o, iv in zip(sls, o_all, inv):
        y_ref[:, sl] = (o * iv * nrm_ref[:, sl] * _sigmoid(zog_ref[:, sl])).astype(BF16)

    @pl.when(pl.program_id(1) == nlast)
    def _():
        for h in heads:
            s_ref[0, h] = st_scr[h].T


def _hgrn(z, lb_all, hb_norm, s0, layer, batch, seq, L):
    nc = seq // L
    w = HB_WIDTH
    blk = lambda col: pl.BlockSpec((L, w), lambda b, c: (b * nc + c, col))
    st = pl.BlockSpec((1, HB_HEADS, HB_DIM, HB_DIM), lambda b, c: (b, 0, 0, 0))
    return pl.pallas_call(
        _hgrn_kernel,
        grid=(batch, nc),
        in_specs=[blk(4), blk(5), blk(6), blk(7),
                  pl.BlockSpec((None, 1, w), lambda b, c: (layer, 0, 0)),
                  pl.BlockSpec((None, 1, w), lambda b, c: (layer, 0, 0)),
                  pl.BlockSpec((None, 1, HB_HEADS, HB_DIM, HB_DIM), lambda b, c: (layer, b, 0, 0, 0))],
        out_specs=[pl.BlockSpec((L, w), lambda b, c: (b * nc + c, 0)), st],
        out_shape=[jax.ShapeDtypeStruct((batch * seq, w), BF16),
                   jax.ShapeDtypeStruct(s0.shape[1:], F32)],
        scratch_shapes=[pltpu.VMEM((HB_HEADS, HB_DIM, HB_DIM), F32),
                        pltpu.VMEM((L, w), F32), pltpu.VMEM((L, w), F32),
                        pltpu.VMEM((L, w), F32), pltpu.VMEM((L, w), F32)],
        compiler_params=_params("parallel", "arbitrary"),
        name="hgrn_scan",
    )(z, z, z, z, lb_all, hb_norm, s0)


def _mm_br_kernel(ya_ref, yb_ref, wa_ref, wb_ref, ga_ref, gb_ref, o_ref):
    ta = _dot(ya_ref[...], wa_ref[...])
    tb = _dot(yb_ref[...], wb_ref[...])
    o_ref[...] = (_sigmoid(ga_ref[...]) * ta + _sigmoid(gb_ref[...]) * tb).astype(BF16)


def _mm_branches(ya, yb, wa, wb, z, ga_col, gb_col, layer, tm=1024, tn=512):
    n, kdim = ya.shape
    d = wa.shape[2]
    tm = min(tm, n)
    ga_blk, gb_blk = ga_col // tn, gb_col // tn
    return pl.pallas_call(
        _mm_br_kernel,
        grid=(n // tm, d // tn),
        in_specs=[pl.BlockSpec((tm, kdim), lambda i, j: (i, 0)),
                  pl.BlockSpec((tm, kdim), lambda i, j: (i, 0)),
                  pl.BlockSpec((None, kdim, tn), lambda i, j: (layer, 0, j)),
                  pl.BlockSpec((None, kdim, tn), lambda i, j: (layer, 0, j)),
                  pl.BlockSpec((tm, tn), lambda i, j: (i, ga_blk + j)),
                  pl.BlockSpec((tm, tn), lambda i, j: (i, gb_blk + j))],
        out_specs=pl.BlockSpec((tm, tn), lambda i, j: (i, j)),
        out_shape=jax.ShapeDtypeStruct((n, d), BF16),
        compiler_params=_params("parallel", "arbitrary"),
        name="branch_merge",
    )(ya, yb, wa, wb, z, z)


def _mm_res_kernel(a_ref, w_ref, x_ref, gt_ref, o_ref):
    o_ref[...] = x_ref[...] + gt_ref[...] * _dot(a_ref[...], w_ref[...])


def _mm_residual(a, w, x, mod, gate_chunk, layer, rows_per_stream, tm=1024, tn=1024):
    n, kdim = a.shape
    d = w.shape[2]
    tm, tps = _row_tile(tm, n, mod, rows_per_stream)
    tn = min(tn, d)
    nj = d // tn
    if mod.ndim == 3:
        gate_spec = pl.BlockSpec((None, 1, tn), lambda i, j: (i // tps, 0, gate_chunk * nj + j))
    else:
        gate_spec = pl.BlockSpec((tm, tn), lambda i, j: (i, gate_chunk * nj + j))
    return pl.pallas_call(
        _mm_res_kernel,
        grid=(n // tm, nj),
        in_specs=[pl.BlockSpec((tm, kdim), lambda i, j: (i, 0)),
                  pl.BlockSpec((None, kdim, tn), lambda i, j: (layer, 0, j)),
                  pl.BlockSpec((tm, tn), lambda i, j: (i, j)),
                  gate_spec],
        out_specs=pl.BlockSpec((tm, tn), lambda i, j: (i, j)),
        out_shape=jax.ShapeDtypeStruct((n, d), F32),
        compiler_params=_params("parallel", "arbitrary"),
        name="out_proj_residual",
    )(a, w, x, mod)


def _ffn_kernel(x_ref, g_ref, sc_ref, sh_ref, gt_ref, wu_ref, wd_ref, fg_ref, o_ref, h_scr, *, final):
    f = pl.program_id(1)

    @pl.when(f == 0)
    def _():
        _norm_mod_rows(x_ref, g_ref, sc_ref, sh_ref, h_scr)
        o_ref[...] = jnp.zeros_like(o_ref)

    u = jnp.square(jnp.maximum(_dot(h_scr[...], wu_ref[...]), 0.0))
    o_ref[...] += _dot(u.astype(BF16), wd_ref[...])

    @pl.when(f == pl.num_programs(1) - 1)
    def _():
        tm = x_ref.shape[0]
        rows = min(tm, 256)

        def body(r, carry):
            sl = pl.ds(pl.multiple_of(r * rows, rows), rows)
            gt = gt_ref[...] if gt_ref.shape[0] == 1 else gt_ref[sl, :]
            xn = x_ref[sl, :] + gt * o_ref[sl, :]
            if final:
                xn = xn * lax.rsqrt(jnp.mean(xn * xn, axis=-1, keepdims=True) + EPS) * fg_ref[...]
            o_ref[sl, :] = xn
            return carry

        lax.fori_loop(0, tm // rows, body, 0)


def _ffn(x, g, mod, wu, wd, final_g, final, layer, rows_per_stream, tm=1024, tf=512):
    n, d = x.shape
    dff = wu.shape[2]
    tm, tps = _row_tile(tm, n, mod, rows_per_stream)
    return pl.pallas_call(
        functools.partial(_ffn_kernel, final=final),
        grid=(n // tm, dff // tf),
        in_specs=[pl.BlockSpec((tm, d), lambda i, j: (i, 0), pipeline_mode=pl.Buffered(1)),
                  pl.BlockSpec((None, 1, d), lambda i, j: (layer, 0, 0)),
                  _mod_spec(mod, 4, d, tm, tps),
                  _mod_spec(mod, 3, d, tm, tps),
                  _mod_spec(mod, 5, d, tm, tps),
                  pl.BlockSpec((None, d, tf), lambda i, j: (layer, 0, j)),
                  pl.BlockSpec((None, tf, d), lambda i, j: (layer, j, 0)),
                  pl.BlockSpec((1, d), lambda i, j: (0, 0))],
        out_specs=pl.BlockSpec((tm, d), lambda i, j: (i, 0)),
        out_shape=jax.ShapeDtypeStruct((n, d), F32),
        scratch_shapes=[pltpu.VMEM((tm, d), BF16)],
        compiler_params=_params("parallel", "arbitrary"),
        name="ffn_residual",
    )(x, g, mod, mod, mod, wu, wd, final_g)


def _trunk(x, mod_all, per_token_mod, conv_c, st_c, st_n, st_m, st_s, lb_all, wts):
    batch, seq, d = x.shape
    depth = conv_c.shape[0]
    L = min(MA_CHUNK, seq)
    Lh = min(HB_CHUNK, seq)
    nc = seq // L
    n = batch * seq
    nz = wts["b_in"].shape[2]
    keep = CONV_W - 1
    xf = x.reshape(n, d)
    bufs, cms, nvs, mrs, sms = [], [], [], [], []
    for l in range(depth):
        mod = mod_all[l]
        if per_token_mod:
            mod = jnp.repeat(mod, seq, axis=0)
        else:
            mod = mod[:, None, :]
        h, zg = _norm_gates(xf, wts["norm1_g"], mod, wts["w_gate"], wts["b_gate"], l, seq)
        z = _mm_in(h, wts["w_in"], wts["b_in"], l, nz)
        gates_t = jnp.swapaxes(zg[:, :2 * MA_HEADS].reshape(batch, nc, L, 2 * MA_HEADS), 2, 3)
        conv0 = jnp.pad(conv_c[l], ((0, 0), (CONV_PAD - keep, 0), (0, 0)))
        m0 = jnp.broadcast_to(st_m[l][:, :, None], (batch, MA_HEADS, GATE_PAD))
        ya, c1, n1, m1 = _mlstm(z, zg, gates_t, wts["conv_w"], wts["conv_b"], wts["ma_norm"],
                                conv0, st_c, st_n, m0, l, batch, seq, L)
        yb, s1 = _hgrn(z, lb_all, wts["hb_norm"], st_s, l, batch, seq, Lh)
        merged = _mm_branches(ya, yb, wts["w_br_a"], wts["w_br_b"], z,
                              4 * MA_WIDTH + 4 * HB_WIDTH, 4 * MA_WIDTH + 4 * HB_WIDTH + d, l)
        xf = _mm_residual(merged, wts["w_o"], xf, mod, 2, l, seq)
        xf = _ffn(xf, wts["norm2_g"], mod, wts["w_up"], wts["w_down"], wts["final_g"],
                  l == depth - 1, l, seq)
        bufs.append(z.reshape(batch, seq, nz)[:, seq - keep:, :2 * MA_WIDTH])
        cms.append(c1)
        nvs.append(n1)
        mrs.append(m1[:, :, 0])
        sms.append(s1)
    return (xf.reshape(batch, seq, d), jnp.stack(bufs), jnp.stack(cms), jnp.stack(nvs),
            jnp.stack(mrs), jnp.stack(sms))


def kernel(x_prompt, x_sample, cache_conv, state_mlstm_C, state_mlstm_n, state_mlstm_m, state_hgrn,
           c_prompt, c_sample, ada_w, ada_b, norm1_g, norm2_g, w_in, b_in, conv_w, conv_b, ma_norm,
           hgrn_lb_raw, hb_norm, w_br_a, w_br_b, w_o, w_up, w_down, final_g):
    depth, d, n_in = w_in.shape
    bp, bs = x_prompt.shape[0], x_sample.shape[0]
    n_main = n_in - 2 * MA_HEADS
    gate_pad = GATE_PAD - 2 * MA_HEADS

    lb_all = _lower_bounds(hgrn_lb_raw).reshape(depth, 1, HB_WIDTH)
    c_all = jnp.concatenate([c_prompt, c_sample], axis=0)
    c_rows = -(-c_all.shape[0] // 8) * 8
    c_all = jnp.pad(c_all, ((0, c_rows - c_all.shape[0]), (0, 0)))
    mod_all = _modulation(c_all, ada_w, ada_b)

    wts = {
        "norm1_g": norm1_g.reshape(depth, 1, d),
        "norm2_g": norm2_g.reshape(depth, 1, d),
        "w_in": _to_bf16(w_in, min(128, w_in.shape[1])),
        "b_in": b_in[:, :n_main].reshape(depth, 1, n_main),
        "w_gate": jnp.pad(w_in[:, :, n_main:], ((0, 0), (0, 0), (0, gate_pad))).astype(BF16),
        "b_gate": jnp.pad(b_in[:, n_main:], ((0, 0), (0, gate_pad))).reshape(depth, 1, GATE_PAD),
        "conv_w": conv_w,
        "conv_b": conv_b.reshape(depth, 1, -1),
        "ma_norm": ma_norm.reshape(depth, 1, -1),
        "hb_norm": hb_norm.reshape(depth, 1, -1),
        "w_br_a": _to_bf16(w_br_a, min(1024, w_br_a.shape[1])),
        "w_br_b": _to_bf16(w_br_b, min(1024, w_br_b.shape[1])),
        "w_o": _to_bf16(w_o, min(1024, w_o.shape[1])),
        "w_up": _to_bf16(w_up, min(256, w_up.shape[1])),
        "w_down": _to_bf16(w_down, min(1024, w_down.shape[1])),
        "final_g": final_g.reshape(1, d),
    }

    f32 = jnp.float32
    z_conv = jnp.zeros((depth, bp, CONV_W - 1, 2 * MA_WIDTH), f32)
    z_c = jnp.zeros((depth, bp, MA_HEADS, MA_HEAD_DIM, MA_HEAD_DIM), f32)
    z_n = jnp.zeros((depth, bp, MA_HEADS, MA_HEAD_DIM), f32)
    z_m = jnp.zeros((depth, bp, MA_HEADS), f32)
    z_s = jnp.zeros((depth, bp, HB_HEADS, HB_DIM, HB_DIM), f32)

    out_p = _trunk(x_prompt, mod_all[:, :bp], False, z_conv, z_c, z_n, z_m, z_s, lb_all, wts)
    out_s = _trunk(x_sample, mod_all[:, bp:bp + bs], True, cache_conv, state_mlstm_C, state_mlstm_n,
                   state_mlstm_m, state_hgrn, lb_all, wts)
    y_p, conv_p, c_p, n_p, m_p, s_p = out_p
    y_s, conv_s, c_s, n_s, m_s, s_s = out_s
    return (y_p, y_s, conv_p, c_p, n_p, m_p, s_p, conv_s, c_s, n_s, m_s, s_s)
```

```python
import functools

import jax
import jax.numpy as jnp
from jax import lax
from jax.experimental import pallas as pl
from jax.experimental.pallas import tpu as pltpu

EPS = 1e-6
NEG = -1e30
MA_CHUNK = 256
HB_CHUNK = 64
HB_STEP_ROWS = 256
MA_HEADS = 8
MA_HEAD_DIM = 256
MA_WIDTH = MA_HEADS * MA_HEAD_DIM
CONV_W = 4
HB_HEADS = 16
HB_DIM = 128
HB_WIDTH = HB_HEADS * HB_DIM
GATE_PAD = 128
CONV_PAD = 8
HGRN_SAFE_EXP = 80.0
VMEM_LIMIT = 56 * 1024 * 1024

F32 = jnp.float32
BF16 = jnp.bfloat16

_NT = (((1,), (1,)), ((), ()))
_TN = (((0,), (0,)), ((), ()))


def _params(*sem):
    return pltpu.CompilerParams(dimension_semantics=sem, vmem_limit_bytes=VMEM_LIMIT)


def _sigmoid(x):
    return 1.0 / (1.0 + jnp.exp(-x))


def _log_sigmoid(x):
    return jnp.minimum(x, 0.0) - jnp.log1p(jnp.exp(-jnp.abs(x)))


def _dot(a, b):
    return jnp.dot(a, b, preferred_element_type=F32)


def _lb_kernel(raw_ref, o_ref):
    x = raw_ref[...]
    e = jnp.exp(x - jnp.max(x, axis=0, keepdims=True))
    sm = e / jnp.sum(e, axis=0, keepdims=True)
    depth = x.shape[0]
    acc = sm[0:1, :]
    o_ref[0:1, :] = acc - sm[0:1, :]
    for l in range(1, depth):
        acc = acc + sm[l:l + 1, :]
        o_ref[l:l + 1, :] = acc - sm[0:1, :]


def _lower_bounds(raw):
    return pl.pallas_call(
        _lb_kernel, out_shape=jax.ShapeDtypeStruct(raw.shape, F32), name="hgrn_lower_bounds")(raw)


def _ada_kernel(c_ref, w_ref, b_ref, o_ref):
    c = c_ref[...]
    cs = (c * _sigmoid(c)).astype(BF16)
    o_ref[...] = _dot(cs, w_ref[...].astype(BF16)) + b_ref[...]


def _modulation(c_all, ada_w, ada_b, tn=1024):
    depth, d, n6 = ada_w.shape
    rows = c_all.shape[0]
    return pl.pallas_call(
        _ada_kernel,
        grid=(depth, n6 // tn),
        in_specs=[pl.BlockSpec((rows, d), lambda l, j: (0, 0)),
                  pl.BlockSpec((None, d, tn), lambda l, j: (l, 0, j)),
                  pl.BlockSpec((None, 1, tn), lambda l, j: (l, 0, j))],
        out_specs=pl.BlockSpec((None, rows, tn), lambda l, j: (l, 0, j)),
        out_shape=jax.ShapeDtypeStruct((depth, rows, n6), F32),
        compiler_params=_params("parallel", "parallel"),
        name="ada_modulation",
    )(c_all, ada_w, ada_b.reshape(depth, 1, n6))


def _norm_mod_rows(x_ref, g_ref, sc_ref, sh_ref, h_scr):
    tm = x_ref.shape[0]
    rows = min(tm, 256)

    def body(r, carry):
        sl = pl.ds(pl.multiple_of(r * rows, rows), rows)
        x = x_ref[sl, :]
        y = x * lax.rsqrt(jnp.mean(x * x, axis=-1, keepdims=True) + EPS) * g_ref[...]
        sc = sc_ref[...] if sc_ref.shape[0] == 1 else sc_ref[sl, :]
        sh = sh_ref[...] if sh_ref.shape[0] == 1 else sh_ref[sl, :]
        h_scr[sl, :] = (y * (1.0 + sc) + sh).astype(BF16)
        return carry

    lax.fori_loop(0, tm // rows, body, 0)


def _mod_spec(mod, k, d, tm, tiles_per_stream):
    if mod.ndim == 3:
        return pl.BlockSpec((None, 1, d), lambda i, j: (i // tiles_per_stream, 0, k))
    return pl.BlockSpec((tm, d), lambda i, j: (i, k))


def _row_tile(tm, n, mod, rows_per_stream):
    tm = min(tm, n if mod.ndim == 2 else rows_per_stream)
    return tm, max(rows_per_stream // tm, 1)


def _norm_kernel(x_ref, g_ref, sc_ref, sh_ref, wg_ref, bg_ref, h_ref, zg_ref):
    _norm_mod_rows(x_ref, g_ref, sc_ref, sh_ref, h_ref)
    zg_ref[...] = _dot(h_ref[...], wg_ref[...]) + bg_ref[...]


def _norm_gates(x, g, mod, wg, bg, layer, rows_per_stream, tm=512):
    n, d = x.shape
    tm, tps = _row_tile(tm, n, mod, rows_per_stream)
    return pl.pallas_call(
        _norm_kernel,
        grid=(n // tm, 1),
        in_specs=[pl.BlockSpec((tm, d), lambda i, j: (i, 0)),
                  pl.BlockSpec((None, 1, d), lambda i, j: (layer, 0, 0)),
                  _mod_spec(mod, 1, d, tm, tps),
                  _mod_spec(mod, 0, d, tm, tps),
                  pl.BlockSpec((None, d, GATE_PAD), lambda i, j: (layer, 0, 0)),
                  pl.BlockSpec((None, 1, GATE_PAD), lambda i, j: (layer, 0, 0))],
        out_specs=[pl.BlockSpec((tm, d), lambda i, j: (i, 0)),
                   pl.BlockSpec((tm, GATE_PAD), lambda i, j: (i, 0))],
        out_shape=[jax.ShapeDtypeStruct((n, d), BF16),
                   jax.ShapeDtypeStruct((n, GATE_PAD), F32)],
        compiler_params=_params("parallel", "arbitrary"),
        name="norm_gates",
    )(x, g, mod, mod, wg, bg)


def _mm_in_kernel(h_ref, w_ref, b_ref, z_ref):
    z_ref[...] = _dot(h_ref[...], w_ref[...]) + b_ref[...]


def _mm_in(h, w, b, layer, nz, tm=2048, tn=1024):
    n, d = h.shape
    tm = min(tm, n)
    return pl.pallas_call(
        _mm_in_kernel,
        grid=(n // tm, nz // tn),
        in_specs=[pl.BlockSpec((tm, d), lambda i, j: (i, 0)),
                  pl.BlockSpec((None, d, tn), lambda i, j: (layer, 0, j)),
                  pl.BlockSpec((None, 1, tn), lambda i, j: (layer, 0, j))],
        out_specs=pl.BlockSpec((tm, tn), lambda i, j: (i, j)),
        out_shape=jax.ShapeDtypeStruct((n, nz), F32),
        compiler_params=_params("parallel", "arbitrary"),
        name="in_proj",
    )(h, w, b)


def _mlstm_kernel(zq_ref, zk_ref, zv_ref, zo_ref, zg_ref, gt_ref, cw_ref, cb_ref, nrm_ref,
                  conv0_ref, c0_ref, n0_ref, m0_ref,
                  y_ref, c_ref, n_ref, m_ref, xq_scr, xk_scr):
    L = zq_ref.shape[0]
    hd = MA_HEAD_DIM
    heads = range(MA_HEADS)
    sls = [slice(h * hd, (h + 1) * hd) for h in heads]

    @pl.when(pl.program_id(1) == 0)
    def _():
        c_ref[...] = c0_ref[...]
        n_ref[...] = n0_ref[...]
        m_ref[...] = m0_ref[...]
        xq_scr[0:CONV_PAD, :] = conv0_ref[0, :, 0:MA_WIDTH]
        xk_scr[0:CONV_PAD, :] = conv0_ref[0, :, MA_WIDTH:2 * MA_WIDTH]

    xq_scr[CONV_PAD:CONV_PAD + L, :] = zq_ref[...]
    xk_scr[CONV_PAD:CONV_PAD + L, :] = zk_ref[...]

    def conv_silu(xs_ref, off):
        xs = xs_ref[...]
        acc = xs * cw_ref[0:1, off:off + MA_WIDTH]
        for j in range(1, CONV_W):
            acc = pltpu.roll(acc, 1, 0) + xs * cw_ref[j:j + 1, off:off + MA_WIDTH]
        acc = acc[CONV_PAD:, :] + cb_ref[:, off:off + MA_WIDTH]
        return acc * _sigmoid(acc)

    q = conv_silu(xq_scr, 0)
    k = conv_silu(xk_scr, MA_WIDTH) * (MA_HEAD_DIM ** -0.5)
    xq_scr[0:CONV_PAD, :] = xq_scr[L:L + CONV_PAD, :]
    xk_scr[0:CONV_PAD, :] = xk_scr[L:L + CONV_PAD, :]
    qb = q.astype(BF16)
    kb = k.astype(BF16)
    vb = zv_ref[...].astype(BF16)

    ti = lax.broadcasted_iota(jnp.int32, (L, L), 0)
    si = lax.broadcasted_iota(jnp.int32, (L, L), 1)
    causal = si <= ti
    gates_t = gt_ref[0, 0]

    d, w_inter, w_last, decay, m_t, m_last = [], [], [], [], [], []
    for h in heads:
        ig_col = zg_ref[:, h:h + 1]
        lf_col = _log_sigmoid(zg_ref[:, MA_HEADS + h:MA_HEADS + h + 1])
        ig_row = gates_t[h:h + 1, :]
        lf_row = _log_sigmoid(gates_t[MA_HEADS + h:MA_HEADS + h + 1, :])
        b_col = jnp.sum(jnp.where(causal, lf_row, 0.0), axis=1, keepdims=True)
        b_row = jnp.sum(jnp.where(ti <= si, lf_col, 0.0), axis=0, keepdims=True)
        log_d = jnp.where(causal, b_col - b_row + ig_row, NEG)
        log_inter = b_col + m_ref[0, h:h + 1, 0:1]
        m_h = jnp.maximum(log_inter, jnp.max(log_d, axis=1, keepdims=True))
        wi = jnp.exp(log_inter - m_h)
        ml = m_h[L - 1:L, :]
        d.append(jnp.exp(log_d - m_h))
        w_inter.append(wi)
        w_last.append(jnp.exp(b_col[L - 1:L, :] - b_col + ig_col - ml))
        decay.append(wi[L - 1:L, :])
        m_t.append(m_h)
        m_last.append(ml)

    s = [lax.dot_general(qb[:, sl], kb[:, sl], _NT, preferred_element_type=F32) * d[h]
         for h, sl in zip(heads, sls)]
    qc = [_dot(qb[:, sl], c_ref[0, h].astype(BF16)) for h, sl in zip(heads, sls)]
    sv = [_dot(s[h].astype(BF16), vb[:, sl]) for h, sl in zip(heads, sls)]
    kw = [k[:, sl] * w_last[h] for h, sl in zip(heads, sls)]
    upd = [lax.dot_general(kw[h].astype(BF16), vb[:, sl], _TN, preferred_element_type=F32)
           for h, sl in zip(heads, sls)]

    for h, sl in zip(heads, sls):
        n_row = n_ref[0, h:h + 1, :]
        qn = (jnp.sum(s[h], axis=1, keepdims=True)
              + w_inter[h] * jnp.sum(q[:, sl] * n_row, axis=1, keepdims=True))
        denom = jnp.maximum(jnp.abs(qn), jnp.exp(-m_t[h]))
        hh = (sv[h] + w_inter[h] * qc[h]) / denom
        y = hh * lax.rsqrt(jnp.mean(hh * hh, axis=-1, keepdims=True) + EPS)
        y = y * nrm_ref[:, sl] * _sigmoid(zo_ref[:, sl])
        y_ref[:, sl] = y.astype(BF16)
        c_ref[0, h] = decay[h] * c_ref[0, h] + upd[h]
        n_ref[0, h:h + 1, :] = decay[h] * n_row + jnp.sum(kw[h], axis=0, keepdims=True)
        m_ref[0, h:h + 1, :] = jnp.broadcast_to(m_last[h], (1, m_ref.shape[2]))


def _mlstm(z, zg, gates_t, conv_w, conv_b, ma_norm, conv0, c0, n0, m0, layer, batch, seq, L):
    nc = seq // L
    w = MA_WIDTH
    blk = lambda col: pl.BlockSpec((L, w), lambda b, c: (b * nc + c, col))
    st4 = pl.BlockSpec((1, MA_HEADS, MA_HEAD_DIM, MA_HEAD_DIM), lambda b, c: (b, 0, 0, 0))
    st3 = pl.BlockSpec((1, MA_HEADS, MA_HEAD_DIM), lambda b, c: (b, 0, 0))
    stm = pl.BlockSpec((1, MA_HEADS, GATE_PAD), lambda b, c: (b, 0, 0))
    return pl.pallas_call(
        _mlstm_kernel,
        grid=(batch, nc),
        in_specs=[blk(0), blk(1), blk(2), blk(3),
                  pl.BlockSpec((L, GATE_PAD), lambda b, c: (b * nc + c, 0)),
                  pl.BlockSpec((1, 1, 2 * MA_HEADS, L), lambda b, c: (b, c, 0, 0)),
                  pl.BlockSpec((None, CONV_W, 2 * w), lambda b, c: (layer, 0, 0)),
                  pl.BlockSpec((None, 1, 2 * w), lambda b, c: (layer, 0, 0)),
                  pl.BlockSpec((None, 1, w), lambda b, c: (layer, 0, 0)),
                  pl.BlockSpec((1, CONV_PAD, 2 * w), lambda b, c: (b, 0, 0)),
                  pl.BlockSpec((None, 1, MA_HEADS, MA_HEAD_DIM, MA_HEAD_DIM),
                               lambda b, c: (layer, b, 0, 0, 0)),
                  pl.BlockSpec((None, 1, MA_HEADS, MA_HEAD_DIM), lambda b, c: (layer, b, 0, 0)),
                  stm],
        out_specs=[pl.BlockSpec((L, w), lambda b, c: (b * nc + c, 0)), st4, st3, stm],
        out_shape=[jax.ShapeDtypeStruct((batch * seq, w), BF16),
                   jax.ShapeDtypeStruct(c0.shape[1:], F32),
                   jax.ShapeDtypeStruct(n0.shape[1:], F32),
                   jax.ShapeDtypeStruct(m0.shape, F32)],
        scratch_shapes=[pltpu.VMEM((CONV_PAD + L, w), F32), pltpu.VMEM((CONV_PAD + L, w), F32)],
        compiler_params=_params("parallel", "arbitrary"),
        name="mlstm_scan",
    )(z, z, z, z, zg, gates_t, conv_w, conv_b, ma_norm, conv0, c0, n0, m0)


def _hgrn_kernel(zq_ref, zf_ref, zi_ref, zog_ref, lb_ref, nrm_ref, s0_ref,
                 y_ref, s_ref, st_scr, b_scr, k_scr, q_scr, o_scr):
    L = b_scr.shape[0]
    hd = HB_DIM
    heads = range(HB_HEADS)
    sls = [slice(h * hd, (h + 1) * hd) for h in heads]
    nlast = pl.num_programs(1) - 1

    @pl.when(pl.program_id(1) == 0)
    def _():
        for h in heads:
            st_scr[h] = s0_ref[0, h].T

    for u in range(zq_ref.shape[0] // L):
        _hgrn_chunk(slice(u * L, (u + 1) * L), zq_ref, zf_ref, zi_ref, zog_ref, lb_ref, nrm_ref,
                    y_ref, st_scr, b_scr, k_scr, q_scr, o_scr)

    @pl.when(pl.program_id(1) == nlast)
    def _():
        for h in heads:
            s_ref[0, h] = st_scr[h].T


def _hgrn_chunk(rows, zq_ref, zf_ref, zi_ref, zog_ref, lb_ref, nrm_ref,
                y_ref, st_scr, b_scr, k_scr, q_scr, o_scr):
    L = b_scr.shape[0]
    hd = HB_DIM
    heads = range(HB_HEADS)
    sls = [slice(h * hd, (h + 1) * hd) for h in heads]

    lb = lb_ref[...]
    fb = zf_ref[rows, :]
    e_f = jnp.exp(-jnp.abs(fb))
    r_f = 1.0 / (1.0 + e_f)
    pos = fb >= 0.0
    log_f = jnp.log(lb + (1.0 - lb) * jnp.where(pos, r_f, e_f * r_f))
    k = (1.0 - lb) * jnp.where(pos, e_f * r_f, r_f)
    qr = zq_ref[rows, :]
    q = qr * _sigmoid(qr)

    ti = lax.broadcasted_iota(jnp.int32, (L, L), 0)
    si = lax.broadcasted_iota(jnp.int32, (L, L), 1)
    causal = si <= ti
    tril = causal.astype(BF16)
    f_hi = log_f.astype(BF16)
    r1 = log_f - f_hi.astype(F32)
    f_mid = r1.astype(BF16)
    f_lo = (r1 - f_mid.astype(F32)).astype(BF16)
    b = _dot(tril, f_hi) + _dot(tril, f_mid) + _dot(tril, f_lo)

    mid = L // 2
    r = b[mid - 1:mid, :]
    b_last = b[L - 1:L, :]
    e_last = jnp.exp(b_last)
    qs = (q * jnp.exp(b)).astype(BF16)
    ks = (k * jnp.exp(b_last - b)).astype(BF16)
    vb = zi_ref[rows, :].astype(BF16)

    inter = [lax.dot_general(qs[:, sl], st_scr[h].astype(BF16), _NT, preferred_element_type=F32)
             for h, sl in zip(heads, sls)]
    upd = [lax.dot_general(vb[:, sl], ks[:, sl], _TN, preferred_element_type=F32) for sl in sls]
    for h, sl in zip(heads, sls):
        o_scr[:, sl] = inter[h]
        st_scr[h] = e_last[:, sl] * st_scr[h] + upd[h]

    spread = jnp.max(jnp.maximum(-r, r - b_last))
    safe = spread <= HGRN_SAFE_EXP

    @pl.when(safe)
    def _():
        qt = (q * jnp.exp(b - r)).astype(BF16)
        kt = (k * jnp.exp(r - b)).astype(BF16)
        a_all = [lax.dot_general(qt[:, sl], kt[:, sl], _NT, preferred_element_type=F32) for sl in sls]
        a_all = [jnp.where(causal, a, 0.0).astype(BF16) for a in a_all]
        intra = [_dot(a, vb[:, sl]) for a, sl in zip(a_all, sls)]
        for sl, o in zip(sls, intra):
            o_scr[:, sl] += o

    @pl.when(jnp.logical_not(safe))
    def _():
        b_scr[...] = b
        k_scr[...] = k
        q_scr[...] = q
        width = b.shape[1]
        head_of_lane = lax.broadcasted_iota(jnp.int32, (width, GATE_PAD), 0) // hd
        col = lax.broadcasted_iota(jnp.int32, (width, GATE_PAD), 1)
        gather = (head_of_lane == col).astype(BF16)
        head_of_lane_t = lax.broadcasted_iota(jnp.int32, (GATE_PAD, width), 1) // hd
        row = lax.broadcasted_iota(jnp.int32, (GATE_PAD, width), 0)
        scatter = (head_of_lane_t == row).astype(BF16)
        t_idx = lax.broadcasted_iota(jnp.int32, (L, GATE_PAD), 0)

        def body(s, carry):
            b_s = b_scr[pl.ds(s, 1), :]
            k_s = k_scr[pl.ds(s, 1), :]
            v_s = zi_ref[pl.ds(rows.start + s, 1), :]
            e = jnp.exp(jnp.minimum(b_scr[...] - b_s, 0.0)) * q_scr[...] * k_s
            a = jnp.where(t_idx >= s, _dot(e.astype(BF16), gather), 0.0)
            o_scr[...] += _dot(a.astype(BF16), scatter) * v_s
            return carry

        lax.fori_loop(0, L, body, 0)

    o_all = [o_scr[:, sl] for sl in sls]
    inv = [lax.rsqrt(jnp.mean(o * o, axis=-1, keepdims=True) + EPS) for o in o_all]
    for sl, o, iv in zip(sls, o_all, inv):
        y_ref[rows, sl] = (o * iv * nrm_ref[:, sl] * _sigmoid(zog_ref[rows, sl])).astype(BF16)


def _hgrn(z, lb_all, hb_norm, s0, layer, batch, seq, L):
    rows = min(HB_STEP_ROWS, seq)
    nc = seq // rows
    w = HB_WIDTH
    blk = lambda col: pl.BlockSpec((rows, w), lambda b, c: (b * nc + c, col))
    st = pl.BlockSpec((1, HB_HEADS, HB_DIM, HB_DIM), lambda b, c: (b, 0, 0, 0))
    return pl.pallas_call(
        _hgrn_kernel,
        grid=(batch, nc),
        in_specs=[blk(4), blk(5), blk(6), blk(7),
                  pl.BlockSpec((None, 1, w), lambda b, c: (layer, 0, 0)),
                  pl.BlockSpec((None, 1, w), lambda b, c: (layer, 0, 0)),
                  pl.BlockSpec((None, 1, HB_HEADS, HB_DIM, HB_DIM), lambda b, c: (layer, b, 0, 0, 0))],
        out_specs=[pl.BlockSpec((rows, w), lambda b, c: (b * nc + c, 0)), st],
        out_shape=[jax.ShapeDtypeStruct((batch * seq, w), BF16),
                   jax.ShapeDtypeStruct(s0.shape[1:], F32)],
        scratch_shapes=[pltpu.VMEM((HB_HEADS, HB_DIM, HB_DIM), F32),
                        pltpu.VMEM((L, w), F32), pltpu.VMEM((L, w), F32),
                        pltpu.VMEM((L, w), F32), pltpu.VMEM((L, w), F32)],
        compiler_params=_params("parallel", "arbitrary"),
        name="hgrn_scan",
    )(z, z, z, z, lb_all, hb_norm, s0)


def _mm_br_kernel(ya_ref, yb_ref, wa_ref, wb_ref, ga_ref, gb_ref, o_ref):
    ta = _dot(ya_ref[...], wa_ref[...])
    tb = _dot(yb_ref[...], wb_ref[...])
    o_ref[...] = (_sigmoid(ga_ref[...]) * ta + _sigmoid(gb_ref[...]) * tb).astype(BF16)


def _mm_branches(ya, yb, wa, wb, z, ga_col, gb_col, layer, tm=1024, tn=512):
    n, kdim = ya.shape
    d = wa.shape[2]
    tm = min(tm, n)
    ga_blk, gb_blk = ga_col // tn, gb_col // tn
    return pl.pallas_call(
        _mm_br_kernel,
        grid=(n // tm, d // tn),
        in_specs=[pl.BlockSpec((tm, kdim), lambda i, j: (i, 0)),
                  pl.BlockSpec((tm, kdim), lambda i, j: (i, 0)),
                  pl.BlockSpec((None, kdim, tn), lambda i, j: (layer, 0, j)),
                  pl.BlockSpec((None, kdim, tn), lambda i, j: (layer, 0, j)),
                  pl.BlockSpec((tm, tn), lambda i, j: (i, ga_blk + j)),
                  pl.BlockSpec((tm, tn), lambda i, j: (i, gb_blk + j))],
        out_specs=pl.BlockSpec((tm, tn), lambda i, j: (i, j)),
        out_shape=jax.ShapeDtypeStruct((n, d), BF16),
        compiler_params=_params("parallel", "arbitrary"),
        name="branch_merge",
    )(ya, yb, wa, wb, z, z)


def _mm_res_kernel(a_ref, w_ref, x_ref, gt_ref, o_ref):
    o_ref[...] = x_ref[...] + gt_ref[...] * _dot(a_ref[...], w_ref[...])


def _mm_residual(a, w, x, mod, gate_chunk, layer, rows_per_stream, tm=1024, tn=1024):
    n, kdim = a.shape
    d = w.shape[2]
    tm, tps = _row_tile(tm, n, mod, rows_per_stream)
    tn = min(tn, d)
    nj = d // tn
    if mod.ndim == 3:
        gate_spec = pl.BlockSpec((None, 1, tn), lambda i, j: (i // tps, 0, gate_chunk * nj + j))
    else:
        gate_spec = pl.BlockSpec((tm, tn), lambda i, j: (i, gate_chunk * nj + j))
    return pl.pallas_call(
        _mm_res_kernel,
        grid=(n // tm, nj),
        in_specs=[pl.BlockSpec((tm, kdim), lambda i, j: (i, 0)),
                  pl.BlockSpec((None, kdim, tn), lambda i, j: (layer, 0, j)),
                  pl.BlockSpec((tm, tn), lambda i, j: (i, j)),
                  gate_spec],
        out_specs=pl.BlockSpec((tm, tn), lambda i, j: (i, j)),
        out_shape=jax.ShapeDtypeStruct((n, d), F32),
        compiler_params=_params("parallel", "arbitrary"),
        name="out_proj_residual",
    )(a, w, x, mod)


def _ffn_kernel(x_ref, g_ref, sc_ref, sh_ref, gt_ref, wu_ref, wd_ref, fg_ref, o_ref, h_scr, *, final):
    f = pl.program_id(1)

    @pl.when(f == 0)
    def _():
        _norm_mod_rows(x_ref, g_ref, sc_ref, sh_ref, h_scr)
        o_ref[...] = jnp.zeros_like(o_ref)

    u = jnp.square(jnp.maximum(_dot(h_scr[...], wu_ref[...]), 0.0))
    o_ref[...] += _dot(u.astype(BF16), wd_ref[...])

    @pl.when(f == pl.num_programs(1) - 1)
    def _():
        tm = x_ref.shape[0]
        rows = min(tm, 256)

        def body(r, carry):
            sl = pl.ds(pl.multiple_of(r * rows, rows), rows)
            gt = gt_ref[...] if gt_ref.shape[0] == 1 else gt_ref[sl, :]
            xn = x_ref[sl, :] + gt * o_ref[sl, :]
            if final:
                xn = xn * lax.rsqrt(jnp.mean(xn * xn, axis=-1, keepdims=True) + EPS) * fg_ref[...]
            o_ref[sl, :] = xn
            return carry

        lax.fori_loop(0, tm // rows, body, 0)


def _ffn(x, g, mod, wu, wd, final_g, final, layer, rows_per_stream, tm=1024, tf=1024):
    n, d = x.shape
    dff = wu.shape[2]
    tm, tps = _row_tile(tm, n, mod, rows_per_stream)
    return pl.pallas_call(
        functools.partial(_ffn_kernel, final=final),
        grid=(n // tm, dff // tf),
        in_specs=[pl.BlockSpec((tm, d), lambda i, j: (i, 0), pipeline_mode=pl.Buffered(1)),
                  pl.BlockSpec((None, 1, d), lambda i, j: (layer, 0, 0)),
                  _mod_spec(mod, 4, d, tm, tps),
                  _mod_spec(mod, 3, d, tm, tps),
                  _mod_spec(mod, 5, d, tm, tps),
                  pl.BlockSpec((None, d, tf), lambda i, j: (layer, 0, j)),
                  pl.BlockSpec((None, tf, d), lambda i, j: (layer, j, 0)),
                  pl.BlockSpec((1, d), lambda i, j: (0, 0))],
        out_specs=pl.BlockSpec((tm, d), lambda i, j: (i, 0)),
        out_shape=jax.ShapeDtypeStruct((n, d), F32),
        scratch_shapes=[pltpu.VMEM((tm, d), BF16)],
        compiler_params=_params("parallel", "arbitrary"),
        name="ffn_residual",
    )(x, g, mod, mod, mod, wu, wd, final_g)


def _trunk(x, mod_all, per_token_mod, conv_c, st_c, st_n, st_m, st_s, lb_all, wts):
    batch, seq, d = x.shape
    depth = conv_c.shape[0]
    L = min(MA_CHUNK, seq)
    Lh = min(HB_CHUNK, seq)
    nc = seq // L
    n = batch * seq
    nz = wts["b_in"].shape[2]
    keep = CONV_W - 1
    xf = x.reshape(n, d)
    bufs, cms, nvs, mrs, sms = [], [], [], [], []
    for l in range(depth):
        mod = mod_all[l]
        if per_token_mod:
            mod = jnp.repeat(mod, seq, axis=0)
        else:
            mod = mod[:, None, :]
        h, zg = _norm_gates(xf, wts["norm1_g"], mod, wts["w_gate"], wts["b_gate"], l, seq)
        z = _mm_in(h, wts["w_in"], wts["b_in"], l, nz)
        gates_t = jnp.swapaxes(zg[:, :2 * MA_HEADS].reshape(batch, nc, L, 2 * MA_HEADS), 2, 3)
        conv0 = jnp.pad(conv_c[l], ((0, 0), (CONV_PAD - keep, 0), (0, 0)))
        m0 = jnp.broadcast_to(st_m[l][:, :, None], (batch, MA_HEADS, GATE_PAD))
        ya, c1, n1, m1 = _mlstm(z, zg, gates_t, wts["conv_w"], wts["conv_b"], wts["ma_norm"],
                                conv0, st_c, st_n, m0, l, batch, seq, L)
        yb, s1 = _hgrn(z, lb_all, wts["hb_norm"], st_s, l, batch, seq, Lh)
        merged = _mm_branches(ya, yb, wts["w_br_a"], wts["w_br_b"], z,
                              4 * MA_WIDTH + 4 * HB_WIDTH, 4 * MA_WIDTH + 4 * HB_WIDTH + d, l)
        xf = _mm_residual(merged, wts["w_o"], xf, mod, 2, l, seq)
        xf = _ffn(xf, wts["norm2_g"], mod, wts["w_up"], wts["w_down"], wts["final_g"],
                  l == depth - 1, l, seq)
        bufs.append(z.reshape(batch, seq, nz)[:, seq - keep:, :2 * MA_WIDTH])
        cms.append(c1)
        nvs.append(n1)
        mrs.append(m1[:, :, 0])
        sms.append(s1)
    return (xf.reshape(batch, seq, d), jnp.stack(bufs), jnp.stack(cms), jnp.stack(nvs),
            jnp.stack(mrs), jnp.stack(sms))


def kernel(x_prompt, x_sample, cache_conv, state_mlstm_C, state_mlstm_n, state_mlstm_m, state_hgrn,
           c_prompt, c_sample, ada_w, ada_b, norm1_g, norm2_g, w_in, b_in, conv_w, conv_b, ma_norm,
           hgrn_lb_raw, hb_norm, w_br_a, w_br_b, w_o, w_up, w_down, final_g):
    depth, d, n_in = w_in.shape
    bp, bs = x_prompt.shape[0], x_sample.shape[0]
    n_main = n_in - 2 * MA_HEADS
    gate_pad = GATE_PAD - 2 * MA_HEADS

    lb_all = _lower_bounds(hgrn_lb_raw).reshape(depth, 1, HB_WIDTH)
    c_all = jnp.concatenate([c_prompt, c_sample], axis=0)
    c_rows = -(-c_all.shape[0] // 8) * 8
    c_all = jnp.pad(c_all, ((0, c_rows - c_all.shape[0]), (0, 0)))
    mod_all = _modulation(c_all, ada_w, ada_b)

    wts = {
        "norm1_g": norm1_g.reshape(depth, 1, d),
        "norm2_g": norm2_g.reshape(depth, 1, d),
        "w_in": w_in.astype(BF16),
        "b_in": b_in[:, :n_main].reshape(depth, 1, n_main),
        "w_gate": jnp.pad(w_in[:, :, n_main:], ((0, 0), (0, 0), (0, gate_pad))).astype(BF16),
        "b_gate": jnp.pad(b_in[:, n_main:], ((0, 0), (0, gate_pad))).reshape(depth, 1, GATE_PAD),
        "conv_w": conv_w,
        "conv_b": conv_b.reshape(depth, 1, -1),
        "ma_norm": ma_norm.reshape(depth, 1, -1),
        "hb_norm": hb_norm.reshape(depth, 1, -1),
        "w_br_a": w_br_a.astype(BF16),
        "w_br_b": w_br_b.astype(BF16),
        "w_o": w_o.astype(BF16),
        "w_up": w_up.astype(BF16),
        "w_down": w_down.astype(BF16),
        "final_g": final_g.reshape(1, d),
    }

    f32 = jnp.float32
    z_conv = jnp.zeros((depth, bp, CONV_W - 1, 2 * MA_WIDTH), f32)
    z_c = jnp.zeros((depth, bp, MA_HEADS, MA_HEAD_DIM, MA_HEAD_DIM), f32)
    z_n = jnp.zeros((depth, bp, MA_HEADS, MA_HEAD_DIM), f32)
    z_m = jnp.zeros((depth, bp, MA_HEADS), f32)
    z_s = jnp.zeros((depth, bp, HB_HEADS, HB_DIM, HB_DIM), f32)

    out_p = _trunk(x_prompt, mod_all[:, :bp], False, z_conv, z_c, z_n, z_m, z_s, lb_all, wts)
    out_s = _trunk(x_sample, mod_all[:, bp:bp + bs], True, cache_conv, state_mlstm_C, state_mlstm_n,
                   state_mlstm_m, state_hgrn, lb_all, wts)
    y_p, conv_p, c_p, n_p, m_p, s_p = out_p
    y_s, conv_s, c_s, n_s, m_s, s_s = out_s
    return (y_p, y_s, conv_p, c_p, n_p, m_p, s_p, conv_s, c_s, n_s, m_s, s_s)
```

```python
import functools

import jax
import jax.numpy as jnp
from jax import lax
from jax.experimental import pallas as pl
from jax.experimental.pallas import tpu as pltpu

EPS = 1e-6
NEG = -1e30
MA_CHUNK = 256
HB_CHUNK = 64
HB_STEP_ROWS = 256
MA_HEADS = 8
MA_HEAD_DIM = 256
MA_WIDTH = MA_HEADS * MA_HEAD_DIM
CONV_W = 4
HB_HEADS = 16
HB_DIM = 128
HB_WIDTH = HB_HEADS * HB_DIM
GATE_PAD = 128
CONV_PAD = 8
STREAM_ROWS = 256
HGRN_SAFE_EXP = 80.0
VMEM_LIMIT = 56 * 1024 * 1024

F32 = jnp.float32
BF16 = jnp.bfloat16

_NT = (((1,), (1,)), ((), ()))
_TN = (((0,), (0,)), ((), ()))


def _params(*sem):
    return pltpu.CompilerParams(dimension_semantics=sem, vmem_limit_bytes=VMEM_LIMIT)


def _sigmoid(x):
    return 1.0 / (1.0 + jnp.exp(-x))


def _log_sigmoid(x):
    return jnp.minimum(x, 0.0) - jnp.log1p(jnp.exp(-jnp.abs(x)))


def _dot(a, b):
    return jnp.dot(a, b, preferred_element_type=F32)


def _lb_kernel(raw_ref, o_ref):
    x = raw_ref[...]
    e = jnp.exp(x - jnp.max(x, axis=0, keepdims=True))
    sm = e / jnp.sum(e, axis=0, keepdims=True)
    depth = x.shape[0]
    acc = sm[0:1, :]
    o_ref[0:1, :] = acc - sm[0:1, :]
    for l in range(1, depth):
        acc = acc + sm[l:l + 1, :]
        o_ref[l:l + 1, :] = acc - sm[0:1, :]


def _lower_bounds(raw):
    return pl.pallas_call(
        _lb_kernel, out_shape=jax.ShapeDtypeStruct(raw.shape, F32), name="hgrn_lower_bounds")(raw)


def _ada_kernel(c_ref, w_ref, b_ref, o_ref):
    c = c_ref[...]
    cs = (c * _sigmoid(c)).astype(BF16)
    o_ref[...] = _dot(cs, w_ref[...].astype(BF16)) + b_ref[...]


def _modulation(c_all, ada_w, ada_b, tn=1024):
    depth, d, n6 = ada_w.shape
    rows = c_all.shape[0]
    return pl.pallas_call(
        _ada_kernel,
        grid=(depth, n6 // tn),
        in_specs=[pl.BlockSpec((rows, d), lambda l, j: (0, 0)),
                  pl.BlockSpec((None, d, tn), lambda l, j: (l, 0, j)),
                  pl.BlockSpec((None, 1, tn), lambda l, j: (l, 0, j))],
        out_specs=pl.BlockSpec((None, rows, tn), lambda l, j: (l, 0, j)),
        out_shape=jax.ShapeDtypeStruct((depth, rows, n6), F32),
        compiler_params=_params("parallel", "parallel"),
        name="ada_modulation",
    )(c_all, ada_w, ada_b.reshape(depth, 1, n6))


def _norm_mod_rows(x_ref, g_ref, sc_ref, sh_ref, h_scr, inv_scr):
    tm = x_ref.shape[0]
    rows = min(tm, 256)

    def stats(r, carry):
        sl = pl.ds(pl.multiple_of(r * rows, rows), rows)
        x = x_ref[sl, :]
        inv_scr[sl, :] = lax.rsqrt(jnp.mean(x * x, axis=-1, keepdims=True) + EPS)
        return carry

    def scale(r, carry):
        sl = pl.ds(pl.multiple_of(r * rows, rows), rows)
        y = x_ref[sl, :] * inv_scr[sl, :] * g_ref[...]
        sc = sc_ref[...] if sc_ref.shape[0] == 1 else sc_ref[sl, :]
        sh = sh_ref[...] if sh_ref.shape[0] == 1 else sh_ref[sl, :]
        h_scr[sl, :] = (y * (1.0 + sc) + sh).astype(BF16)
        return carry

    lax.fori_loop(0, tm // rows, stats, 0)
    lax.fori_loop(0, tm // rows, scale, 0)


def _mod_spec(mod, k, d, tm, tiles_per_stream):
    if mod.ndim == 3:
        return pl.BlockSpec((None, 1, d), lambda i, j: (i // tiles_per_stream, 0, k))
    return pl.BlockSpec((tm, d), lambda i, j: (i, k))


def _col_tile(tn, rows, width, wide):
    t = min(wide if rows <= STREAM_ROWS else tn, width)
    while width % t:
        t //= 2
    return t


def _row_tile(tm, n, mod, rows_per_stream):
    tm = min(tm, n if mod.ndim == 2 else rows_per_stream)
    return tm, max(rows_per_stream // tm, 1)


def _norm_kernel(x_ref, g_ref, sc_ref, sh_ref, wg_ref, bg_ref, h_ref, zg_ref, inv_scr):
    _norm_mod_rows(x_ref, g_ref, sc_ref, sh_ref, h_ref, inv_scr)
    zg_ref[...] = _dot(h_ref[...], wg_ref[...]) + bg_ref[...]


def _norm_gates(x, g, mod, wg, bg, layer, rows_per_stream, tm=512):
    n, d = x.shape
    tm, tps = _row_tile(tm, n, mod, rows_per_stream)
    return pl.pallas_call(
        _norm_kernel,
        grid=(n // tm, 1),
        in_specs=[pl.BlockSpec((tm, d), lambda i, j: (i, 0)),
                  pl.BlockSpec((None, 1, d), lambda i, j: (layer, 0, 0)),
                  _mod_spec(mod, 1, d, tm, tps),
                  _mod_spec(mod, 0, d, tm, tps),
                  pl.BlockSpec((None, d, GATE_PAD), lambda i, j: (layer, 0, 0)),
                  pl.BlockSpec((None, 1, GATE_PAD), lambda i, j: (layer, 0, 0))],
        out_specs=[pl.BlockSpec((tm, d), lambda i, j: (i, 0)),
                   pl.BlockSpec((tm, GATE_PAD), lambda i, j: (i, 0))],
        out_shape=[jax.ShapeDtypeStruct((n, d), BF16),
                   jax.ShapeDtypeStruct((n, GATE_PAD), F32)],
        scratch_shapes=[pltpu.VMEM((tm, 1), F32)],
        compiler_params=_params("parallel", "arbitrary"),
        name="norm_gates",
    )(x, g, mod, mod, wg, bg)


def _mm_in_kernel(h_ref, w_ref, b_ref, z_ref):
    z_ref[...] = _dot(h_ref[...], w_ref[...]) + b_ref[...]


def _mm_in(h, w, b, layer, nz, tm=2048, tn=1024):
    n, d = h.shape
    tm = min(tm, n)
    tn = _col_tile(tn, n, nz, 4096)
    return pl.pallas_call(
        _mm_in_kernel,
        grid=(n // tm, nz // tn),
        in_specs=[pl.BlockSpec((tm, d), lambda i, j: (i, 0)),
                  pl.BlockSpec((None, d, tn), lambda i, j: (layer, 0, j)),
                  pl.BlockSpec((None, 1, tn), lambda i, j: (layer, 0, j))],
        out_specs=pl.BlockSpec((tm, tn), lambda i, j: (i, j)),
        out_shape=jax.ShapeDtypeStruct((n, nz), F32),
        compiler_params=_params("parallel", "arbitrary"),
        name="in_proj",
    )(h, w, b)


def _mlstm_kernel(zq_ref, zk_ref, zv_ref, zo_ref, zg_ref, gt_ref, cw_ref, cb_ref, nrm_ref,
                  conv0_ref, c0_ref, n0_ref, m0_ref,
                  y_ref, c_ref, n_ref, m_ref, hq_scr, hk_scr):
    L = zq_ref.shape[0]
    hd = MA_HEAD_DIM
    heads = range(MA_HEADS)
    sls = [slice(h * hd, (h + 1) * hd) for h in heads]

    @pl.when(pl.program_id(1) == 0)
    def _():
        c_ref[...] = c0_ref[...]
        n_ref[...] = n0_ref[...]
        m_ref[...] = m0_ref[...]
        hq_scr[...] = conv0_ref[0, :, 0:MA_WIDTH]
        hk_scr[...] = conv0_ref[0, :, MA_WIDTH:2 * MA_WIDTH]

    def conv_silu(x_ref, hist_scr, off):
        xs = jnp.concatenate([hist_scr[...], x_ref[...]], axis=0)
        acc = xs * cw_ref[0:1, off:off + MA_WIDTH]
        for j in range(1, CONV_W):
            acc = pltpu.roll(acc, 1, 0) + xs * cw_ref[j:j + 1, off:off + MA_WIDTH]
        acc = acc[CONV_PAD:, :] + cb_ref[:, off:off + MA_WIDTH]
        hist_scr[...] = x_ref[L - CONV_PAD:L, :]
        return acc * _sigmoid(acc)

    q = conv_silu(zq_ref, hq_scr, 0)
    k = conv_silu(zk_ref, hk_scr, MA_WIDTH) * (MA_HEAD_DIM ** -0.5)
    qb = q.astype(BF16)
    kb = k.astype(BF16)
    vb = zv_ref[...].astype(BF16)
    zg = zg_ref[...]
    gates_t = gt_ref[0, 0]

    ti = lax.broadcasted_iota(jnp.int32, (L, L), 0)
    si = lax.broadcasted_iota(jnp.int32, (L, L), 1)
    causal = si <= ti
    tril = causal.astype(BF16)

    def split3(x):
        hi = x.astype(BF16)
        r1 = x - hi.astype(F32)
        mid = r1.astype(BF16)
        return hi, mid, (r1 - mid.astype(F32)).astype(BF16)

    b_cols = sum(_dot(tril, part) for part in split3(_log_sigmoid(zg)))
    b_rows = sum(lax.dot_general(part, tril, _NT, preferred_element_type=F32)
                 for part in split3(_log_sigmoid(gates_t)))

    d, w_inter, w_last, decay, m_t, m_last = [], [], [], [], [], []
    for h in heads:
        ig_col = zg[:, h:h + 1]
        ig_row = gates_t[h:h + 1, :]
        b_col = b_cols[:, MA_HEADS + h:MA_HEADS + h + 1]
        b_row = b_rows[MA_HEADS + h:MA_HEADS + h + 1, :]
        log_d = jnp.where(causal, b_col - b_row + ig_row, NEG)
        log_inter = b_col + m_ref[0, h:h + 1, 0:1]
        m_h = jnp.maximum(log_inter, jnp.max(log_d, axis=1, keepdims=True))
        wi = jnp.exp(log_inter - m_h)
        ml = m_h[L - 1:L, :]
        d.append(jnp.exp(log_d - m_h))
        w_inter.append(wi)
        w_last.append(jnp.exp(b_col[L - 1:L, :] - b_col + ig_col - ml))
        decay.append(wi[L - 1:L, :])
        m_t.append(m_h)
        m_last.append(ml)

    s = [lax.dot_general(qb[:, sl], kb[:, sl], _NT, preferred_element_type=F32) * d[h]
         for h, sl in zip(heads, sls)]
    qc = [_dot(qb[:, sl], c_ref[0, h].astype(BF16)) for h, sl in zip(heads, sls)]
    sv = [_dot(s[h].astype(BF16), vb[:, sl]) for h, sl in zip(heads, sls)]
    kw = [k[:, sl] * w_last[h] for h, sl in zip(heads, sls)]
    upd = [lax.dot_general(kw[h].astype(BF16), vb[:, sl], _TN, preferred_element_type=F32)
           for h, sl in zip(heads, sls)]

    for h, sl in zip(heads, sls):
        n_row = n_ref[0, h:h + 1, :]
        qn = (jnp.sum(s[h], axis=1, keepdims=True)
              + w_inter[h] * jnp.sum(q[:, sl] * n_row, axis=1, keepdims=True))
        denom = jnp.maximum(jnp.abs(qn), jnp.exp(-m_t[h]))
        num = sv[h] + w_inter[h] * qc[h]
        y = num * lax.rsqrt(jnp.mean(num * num, axis=-1, keepdims=True) + EPS * denom * denom)
        y_ref[:, sl] = (y * nrm_ref[:, sl] * _sigmoid(zo_ref[:, sl])).astype(BF16)
        c_ref[0, h] = decay[h] * c_ref[0, h] + upd[h]
        n_ref[0, h:h + 1, :] = decay[h] * n_row + jnp.sum(kw[h], axis=0, keepdims=True)
        m_ref[0, h:h + 1, :] = jnp.broadcast_to(m_last[h], (1, m_ref.shape[2]))


def _mlstm(z, zg, gates_t, conv_w, conv_b, ma_norm, conv0, c0, n0, m0, layer, batch, seq, L):
    nc = seq // L
    w = MA_WIDTH
    blk = lambda col: pl.BlockSpec((L, w), lambda b, c: (b * nc + c, col))
    st4 = pl.BlockSpec((1, MA_HEADS, MA_HEAD_DIM, MA_HEAD_DIM), lambda b, c: (b, 0, 0, 0))
    st3 = pl.BlockSpec((1, MA_HEADS, MA_HEAD_DIM), lambda b, c: (b, 0, 0))
    stm = pl.BlockSpec((1, MA_HEADS, GATE_PAD), lambda b, c: (b, 0, 0))
    return pl.pallas_call(
        _mlstm_kernel,
        grid=(batch, nc),
        in_specs=[blk(0), blk(1), blk(2), blk(3),
                  pl.BlockSpec((L, GATE_PAD), lambda b, c: (b * nc + c, 0)),
                  pl.BlockSpec((1, 1, 2 * MA_HEADS, L), lambda b, c: (b, c, 0, 0)),
                  pl.BlockSpec((None, CONV_W, 2 * w), lambda b, c: (layer, 0, 0)),
                  pl.BlockSpec((None, 1, 2 * w), lambda b, c: (layer, 0, 0)),
                  pl.BlockSpec((None, 1, w), lambda b, c: (layer, 0, 0)),
                  pl.BlockSpec((1, CONV_PAD, 2 * w), lambda b, c: (b, 0, 0)),
                  pl.BlockSpec((None, 1, MA_HEADS, MA_HEAD_DIM, MA_HEAD_DIM),
                               lambda b, c: (layer, b, 0, 0, 0)),
                  pl.BlockSpec((None, 1, MA_HEADS, MA_HEAD_DIM), lambda b, c: (layer, b, 0, 0)),
                  stm],
        out_specs=[pl.BlockSpec((L, w), lambda b, c: (b * nc + c, 0)), st4, st3, stm],
        out_shape=[jax.ShapeDtypeStruct((batch * seq, w), BF16),
                   jax.ShapeDtypeStruct(c0.shape[1:], F32),
                   jax.ShapeDtypeStruct(n0.shape[1:], F32),
                   jax.ShapeDtypeStruct(m0.shape, F32)],
        scratch_shapes=[pltpu.VMEM((CONV_PAD, w), F32), pltpu.VMEM((CONV_PAD, w), F32)],
        compiler_params=_params("parallel", "arbitrary"),
        name="mlstm_scan",
    )(z, z, z, z, zg, gates_t, conv_w, conv_b, ma_norm, conv0, c0, n0, m0)


def _hgrn_kernel(zq_ref, zf_ref, zi_ref, zog_ref, lb_ref, nrm_ref, s0_ref,
                 y_ref, s_ref, st_scr, b_scr, k_scr, q_scr, o_scr):
    L = b_scr.shape[0]
    hd = HB_DIM
    heads = range(HB_HEADS)
    sls = [slice(h * hd, (h + 1) * hd) for h in heads]
    nlast = pl.num_programs(1) - 1

    @pl.when(pl.program_id(1) == 0)
    def _():
        for h in heads:
            st_scr[h] = s0_ref[0, h].T

    for u in range(zq_ref.shape[0] // L):
        _hgrn_chunk(slice(u * L, (u + 1) * L), zq_ref, zf_ref, zi_ref, zog_ref, lb_ref, nrm_ref,
                    y_ref, st_scr, b_scr, k_scr, q_scr, o_scr)

    @pl.when(pl.program_id(1) == nlast)
    def _():
        for h in heads:
            s_ref[0, h] = st_scr[h].T


def _hgrn_chunk(rows, zq_ref, zf_ref, zi_ref, zog_ref, lb_ref, nrm_ref,
                y_ref, st_scr, b_scr, k_scr, q_scr, o_scr):
    L = b_scr.shape[0]
    hd = HB_DIM
    heads = range(HB_HEADS)
    sls = [slice(h * hd, (h + 1) * hd) for h in heads]

    lb = lb_ref[...]
    fb = zf_ref[rows, :]
    e_f = jnp.exp(-jnp.abs(fb))
    r_f = 1.0 / (1.0 + e_f)
    pos = fb >= 0.0
    log_f = jnp.log(lb + (1.0 - lb) * jnp.where(pos, r_f, e_f * r_f))
    k = (1.0 - lb) * jnp.where(pos, e_f * r_f, r_f)
    qr = zq_ref[rows, :]
    q = qr * _sigmoid(qr)

    ti = lax.broadcasted_iota(jnp.int32, (L, L), 0)
    si = lax.broadcasted_iota(jnp.int32, (L, L), 1)
    causal = si <= ti
    tril = causal.astype(BF16)
    f_hi = log_f.astype(BF16)
    r1 = log_f - f_hi.astype(F32)
    f_mid = r1.astype(BF16)
    f_lo = (r1 - f_mid.astype(F32)).astype(BF16)
    b = _dot(tril, f_hi) + _dot(tril, f_mid) + _dot(tril, f_lo)

    mid = L // 2
    r = b[mid - 1:mid, :]
    b_last = b[L - 1:L, :]
    e_last = jnp.exp(b_last)
    qs = (q * jnp.exp(b)).astype(BF16)
    ks = (k * jnp.exp(b_last - b)).astype(BF16)
    vb = zi_ref[rows, :].astype(BF16)

    inter = [lax.dot_general(qs[:, sl], st_scr[h].astype(BF16), _NT, preferred_element_type=F32)
             for h, sl in zip(heads, sls)]
    upd = [lax.dot_general(vb[:, sl], ks[:, sl], _TN, preferred_element_type=F32) for sl in sls]
    for h, sl in zip(heads, sls):
        o_scr[:, sl] = inter[h]
        st_scr[h] = e_last[:, sl] * st_scr[h] + upd[h]

    spread = jnp.max(jnp.maximum(-r, r - b_last))
    safe = spread <= HGRN_SAFE_EXP

    @pl.when(safe)
    def _():
        qt = (q * jnp.exp(b - r)).astype(BF16)
        kt = (k * jnp.exp(r - b)).astype(BF16)
        a_all = [lax.dot_general(qt[:, sl], kt[:, sl], _NT, preferred_element_type=F32) for sl in sls]
        a_all = [jnp.where(causal, a, 0.0).astype(BF16) for a in a_all]
        intra = [_dot(a, vb[:, sl]) for a, sl in zip(a_all, sls)]
        for sl, o in zip(sls, intra):
            o_scr[:, sl] += o

    @pl.when(jnp.logical_not(safe))
    def _():
        b_scr[...] = b
        k_scr[...] = k
        q_scr[...] = q
        width = b.shape[1]
        head_of_lane = lax.broadcasted_iota(jnp.int32, (width, GATE_PAD), 0) // hd
        col = lax.broadcasted_iota(jnp.int32, (width, GATE_PAD), 1)
        gather = (head_of_lane == col).astype(BF16)
        head_of_lane_t = lax.broadcasted_iota(jnp.int32, (GATE_PAD, width), 1) // hd
        row = lax.broadcasted_iota(jnp.int32, (GATE_PAD, width), 0)
        scatter = (head_of_lane_t == row).astype(BF16)
        t_idx = lax.broadcasted_iota(jnp.int32, (L, GATE_PAD), 0)

        def body(s, carry):
            b_s = b_scr[pl.ds(s, 1), :]
            k_s = k_scr[pl.ds(s, 1), :]
            v_s = zi_ref[pl.ds(rows.start + s, 1), :]
            e = jnp.exp(jnp.minimum(b_scr[...] - b_s, 0.0)) * q_scr[...] * k_s
            a = jnp.where(t_idx >= s, _dot(e.astype(BF16), gather), 0.0)
            o_scr[...] += _dot(a.astype(BF16), scatter) * v_s
            return carry

        lax.fori_loop(0, L, body, 0)

    o_all = [o_scr[:, sl] for sl in sls]
    inv = [lax.rsqrt(jnp.mean(o * o, axis=-1, keepdims=True) + EPS) for o in o_all]
    for sl, o, iv in zip(sls, o_all, inv):
        y_ref[rows, sl] = (o * iv * nrm_ref[:, sl] * _sigmoid(zog_ref[rows, sl])).astype(BF16)


def _hgrn(z, lb_all, hb_norm, s0, layer, batch, seq, L):
    rows = min(HB_STEP_ROWS, seq)
    nc = seq // rows
    w = HB_WIDTH
    blk = lambda col: pl.BlockSpec((rows, w), lambda b, c: (b * nc + c, col))
    st = pl.BlockSpec((1, HB_HEADS, HB_DIM, HB_DIM), lambda b, c: (b, 0, 0, 0))
    return pl.pallas_call(
        _hgrn_kernel,
        grid=(batch, nc),
        in_specs=[blk(4), blk(5), blk(6), blk(7),
                  pl.BlockSpec((None, 1, w), lambda b, c: (layer, 0, 0)),
                  pl.BlockSpec((None, 1, w), lambda b, c: (layer, 0, 0)),
                  pl.BlockSpec((None, 1, HB_HEADS, HB_DIM, HB_DIM), lambda b, c: (layer, b, 0, 0, 0))],
        out_specs=[pl.BlockSpec((rows, w), lambda b, c: (b * nc + c, 0)), st],
        out_shape=[jax.ShapeDtypeStruct((batch * seq, w), BF16),
                   jax.ShapeDtypeStruct(s0.shape[1:], F32)],
        scratch_shapes=[pltpu.VMEM((HB_HEADS, HB_DIM, HB_DIM), F32),
                        pltpu.VMEM((L, w), F32), pltpu.VMEM((L, w), F32),
                        pltpu.VMEM((L, w), F32), pltpu.VMEM((L, w), F32)],
        compiler_params=_params("parallel", "arbitrary"),
        name="hgrn_scan",
    )(z, z, z, z, lb_all, hb_norm, s0)


def _mm_br_kernel(ya_ref, yb_ref, wa_ref, wb_ref, ga_ref, gb_ref, o_ref):
    ta = _dot(ya_ref[...], wa_ref[...])
    tb = _dot(yb_ref[...], wb_ref[...])
    o_ref[...] = (_sigmoid(ga_ref[...]) * ta + _sigmoid(gb_ref[...]) * tb).astype(BF16)


def _mm_branches(ya, yb, wa, wb, z, ga_col, gb_col, layer, tm=1024, tn=512):
    n, kdim = ya.shape
    d = wa.shape[2]
    tm = min(tm, n)
    tn = _col_tile(tn, n, d, 2048)
    ga_blk, gb_blk = ga_col // tn, gb_col // tn
    return pl.pallas_call(
        _mm_br_kernel,
        grid=(n // tm, d // tn),
        in_specs=[pl.BlockSpec((tm, kdim), lambda i, j: (i, 0)),
                  pl.BlockSpec((tm, kdim), lambda i, j: (i, 0)),
                  pl.BlockSpec((None, kdim, tn), lambda i, j: (layer, 0, j)),
                  pl.BlockSpec((None, kdim, tn), lambda i, j: (layer, 0, j)),
                  pl.BlockSpec((tm, tn), lambda i, j: (i, ga_blk + j)),
                  pl.BlockSpec((tm, tn), lambda i, j: (i, gb_blk + j))],
        out_specs=pl.BlockSpec((tm, tn), lambda i, j: (i, j)),
        out_shape=jax.ShapeDtypeStruct((n, d), BF16),
        compiler_params=_params("parallel", "arbitrary"),
        name="branch_merge",
    )(ya, yb, wa, wb, z, z)


def _mm_res_kernel(a_ref, w_ref, x_ref, gt_ref, o_ref):
    o_ref[...] = x_ref[...] + gt_ref[...] * _dot(a_ref[...], w_ref[...])


def _mm_residual(a, w, x, mod, gate_chunk, layer, rows_per_stream, tm=1024, tn=1024):
    n, kdim = a.shape
    d = w.shape[2]
    tm, tps = _row_tile(tm, n, mod, rows_per_stream)
    tn = _col_tile(tn, n, d, 2048)
    nj = d // tn
    if mod.ndim == 3:
        gate_spec = pl.BlockSpec((None, 1, tn), lambda i, j: (i // tps, 0, gate_chunk * nj + j))
    else:
        gate_spec = pl.BlockSpec((tm, tn), lambda i, j: (i, gate_chunk * nj + j))
    return pl.pallas_call(
        _mm_res_kernel,
        grid=(n // tm, nj),
        in_specs=[pl.BlockSpec((tm, kdim), lambda i, j: (i, 0)),
                  pl.BlockSpec((None, kdim, tn), lambda i, j: (layer, 0, j)),
                  pl.BlockSpec((tm, tn), lambda i, j: (i, j)),
                  gate_spec],
        out_specs=pl.BlockSpec((tm, tn), lambda i, j: (i, j)),
        out_shape=jax.ShapeDtypeStruct((n, d), F32),
        compiler_params=_params("parallel", "arbitrary"),
        name="out_proj_residual",
    )(a, w, x, mod)


def _ffn_kernel(x_ref, g_ref, sc_ref, sh_ref, gt_ref, wu_ref, wd_ref, fg_ref, o_ref, h_scr, inv_scr,
                *, final):
    f = pl.program_id(1)

    @pl.when(f == 0)
    def _():
        _norm_mod_rows(x_ref, g_ref, sc_ref, sh_ref, h_scr, inv_scr)
        o_ref[...] = jnp.zeros_like(o_ref)

    u = jnp.square(jnp.maximum(_dot(h_scr[...], wu_ref[...]), 0.0))
    o_ref[...] += _dot(u.astype(BF16), wd_ref[...])

    @pl.when(f == pl.num_programs(1) - 1)
    def _():
        tm = x_ref.shape[0]
        rows = min(tm, 256)

        def body(r, carry):
            sl = pl.ds(pl.multiple_of(r * rows, rows), rows)
            gt = gt_ref[...] if gt_ref.shape[0] == 1 else gt_ref[sl, :]
            xn = x_ref[sl, :] + gt * o_ref[sl, :]
            if final:
                xn = xn * lax.rsqrt(jnp.mean(xn * xn, axis=-1, keepdims=True) + EPS) * fg_ref[...]
            o_ref[sl, :] = xn
            return carry

        lax.fori_loop(0, tm // rows, body, 0)


def _ffn(x, g, mod, wu, wd, final_g, final, layer, rows_per_stream, tm=1024, tf=1024):
    n, d = x.shape
    dff = wu.shape[2]
    tm, tps = _row_tile(tm, n, mod, rows_per_stream)
    tf = _col_tile(tf, n, dff, 2048)
    return pl.pallas_call(
        functools.partial(_ffn_kernel, final=final),
        grid=(n // tm, dff // tf),
        in_specs=[pl.BlockSpec((tm, d), lambda i, j: (i, 0), pipeline_mode=pl.Buffered(1)),
                  pl.BlockSpec((None, 1, d), lambda i, j: (layer, 0, 0)),
                  _mod_spec(mod, 4, d, tm, tps),
                  _mod_spec(mod, 3, d, tm, tps),
                  _mod_spec(mod, 5, d, tm, tps),
                  pl.BlockSpec((None, d, tf), lambda i, j: (layer, 0, j)),
                  pl.BlockSpec((None, tf, d), lambda i, j: (layer, j, 0)),
                  pl.BlockSpec((1, d), lambda i, j: (0, 0))],
        out_specs=pl.BlockSpec((tm, d), lambda i, j: (i, 0)),
        out_shape=jax.ShapeDtypeStruct((n, d), F32),
        scratch_shapes=[pltpu.VMEM((tm, d), BF16), pltpu.VMEM((tm, 1), F32)],
        compiler_params=_params("parallel", "arbitrary"),
        name="ffn_residual",
    )(x, g, mod, mod, mod, wu, wd, final_g)


def _trunk(x, mod_all, per_token_mod, conv_c, st_c, st_n, st_m, st_s, lb_all, wts):
    batch, seq, d = x.shape
    depth = conv_c.shape[0]
    L = min(MA_CHUNK, seq)
    Lh = min(HB_CHUNK, seq)
    nc = seq // L
    n = batch * seq
    nz = wts["b_in"].shape[2]
    keep = CONV_W - 1
    xf = x.reshape(n, d)
    bufs, cms, nvs, mrs, sms = [], [], [], [], []
    for l in range(depth):
        mod = mod_all[l]
        if per_token_mod:
            mod = jnp.repeat(mod, seq, axis=0)
        else:
            mod = mod[:, None, :]
        h, zg = _norm_gates(xf, wts["norm1_g"], mod, wts["w_gate"], wts["b_gate"], l, seq)
        z = _mm_in(h, wts["w_in"], wts["b_in"], l, nz)
        gates_t = jnp.swapaxes(zg[:, :2 * MA_HEADS].reshape(batch, nc, L, 2 * MA_HEADS), 2, 3)
        conv0 = jnp.pad(conv_c[l], ((0, 0), (CONV_PAD - keep, 0), (0, 0)))
        m0 = jnp.broadcast_to(st_m[l][:, :, None], (batch, MA_HEADS, GATE_PAD))
        ya, c1, n1, m1 = _mlstm(z, zg, gates_t, wts["conv_w"], wts["conv_b"], wts["ma_norm"],
                                conv0, st_c, st_n, m0, l, batch, seq, L)
        yb, s1 = _hgrn(z, lb_all, wts["hb_norm"], st_s, l, batch, seq, Lh)
        merged = _mm_branches(ya, yb, wts["w_br_a"], wts["w_br_b"], z,
                              4 * MA_WIDTH + 4 * HB_WIDTH, 4 * MA_WIDTH + 4 * HB_WIDTH + d, l)
        xf = _mm_residual(merged, wts["w_o"], xf, mod, 2, l, seq)
        xf = _ffn(xf, wts["norm2_g"], mod, wts["w_up"], wts["w_down"], wts["final_g"],
                  l == depth - 1, l, seq)
        bufs.append(z.reshape(batch, seq, nz)[:, seq - keep:, :2 * MA_WIDTH])
        cms.append(c1)
        nvs.append(n1)
        mrs.append(m1[:, :, 0])
        sms.append(s1)
    return (xf.reshape(batch, seq, d), jnp.stack(bufs), jnp.stack(cms), jnp.stack(nvs),
            jnp.stack(mrs), jnp.stack(sms))


def kernel(x_prompt, x_sample, cache_conv, state_mlstm_C, state_mlstm_n, state_mlstm_m, state_hgrn,
           c_prompt, c_sample, ada_w, ada_b, norm1_g, norm2_g, w_in, b_in, conv_w, conv_b, ma_norm,
           hgrn_lb_raw, hb_norm, w_br_a, w_br_b, w_o, w_up, w_down, final_g):
    depth, d, n_in = w_in.shape
    bp, bs = x_prompt.shape[0], x_sample.shape[0]
    n_main = n_in - 2 * MA_HEADS
    gate_pad = GATE_PAD - 2 * MA_HEADS

    lb_all = _lower_bounds(hgrn_lb_raw).reshape(depth, 1, HB_WIDTH)
    c_all = jnp.concatenate([c_prompt, c_sample], axis=0)
    c_rows = -(-c_all.shape[0] // 8) * 8
    c_all = jnp.pad(c_all, ((0, c_rows - c_all.shape[0]), (0, 0)))
    mod_all = _modulation(c_all, ada_w, ada_b)

    wts = {
        "norm1_g": norm1_g.reshape(depth, 1, d),
        "norm2_g": norm2_g.reshape(depth, 1, d),
        "w_in": w_in.astype(BF16),
        "b_in": b_in[:, :n_main].reshape(depth, 1, n_main),
        "w_gate": jnp.pad(w_in[:, :, n_main:], ((0, 0), (0, 0), (0, gate_pad))).astype(BF16),
        "b_gate": jnp.pad(b_in[:, n_main:], ((0, 0), (0, gate_pad))).reshape(depth, 1, GATE_PAD),
        "conv_w": conv_w,
        "conv_b": conv_b.reshape(depth, 1, -1),
        "ma_norm": ma_norm.reshape(depth, 1, -1),
        "hb_norm": hb_norm.reshape(depth, 1, -1),
        "w_br_a": w_br_a.astype(BF16),
        "w_br_b": w_br_b.astype(BF16),
        "w_o": w_o.astype(BF16),
        "w_up": w_up.astype(BF16),
        "w_down": w_down.astype(BF16),
        "final_g": final_g.reshape(1, d),
    }

    f32 = jnp.float32
    z_conv = jnp.zeros((depth, bp, CONV_W - 1, 2 * MA_WIDTH), f32)
    z_c = jnp.zeros((depth, bp, MA_HEADS, MA_HEAD_DIM, MA_HEAD_DIM), f32)
    z_n = jnp.zeros((depth, bp, MA_HEADS, MA_HEAD_DIM), f32)
    z_m = jnp.zeros((depth, bp, MA_HEADS), f32)
    z_s = jnp.zeros((depth, bp, HB_HEADS, HB_DIM, HB_DIM), f32)

    out_p = _trunk(x_prompt, mod_all[:, :bp], False, z_conv, z_c, z_n, z_m, z_s, lb_all, wts)
    out_s = _trunk(x_sample, mod_all[:, bp:bp + bs], True, cache_conv, state_mlstm_C, state_mlstm_n,
                   state_mlstm_m, state_hgrn, lb_all, wts)
    y_p, conv_p, c_p, n_p, m_p, s_p = out_p
    y_s, conv_s, c_s, n_s, m_s, s_s = out_s
    return (y_p, y_s, conv_p, c_p, n_p, m_p, s_p, conv_s, c_s, n_s, m_s, s_s)
```

```python
import functools

import jax
import jax.numpy as jnp
from jax import lax
from jax.experimental import pallas as pl
from jax.experimental.pallas import tpu as pltpu

EPS = 1e-6
NEG = -1e30
MA_CHUNK = 256
HB_CHUNK = 64
HB_STEP_ROWS = 256
MA_HEADS = 8
MA_HEAD_DIM = 256
MA_WIDTH = MA_HEADS * MA_HEAD_DIM
CONV_W = 4
HB_HEADS = 16
HB_DIM = 128
HB_WIDTH = HB_HEADS * HB_DIM
GATE_PAD = 128
CONV_PAD = 8
STREAM_ROWS = 256
HGRN_SAFE_EXP = 80.0
VMEM_LIMIT = 56 * 1024 * 1024

F32 = jnp.float32
BF16 = jnp.bfloat16

_NT = (((1,), (1,)), ((), ()))
_TN = (((0,), (0,)), ((), ()))


def _params(*sem):
    return pltpu.CompilerParams(dimension_semantics=sem, vmem_limit_bytes=VMEM_LIMIT)


def _sigmoid(x):
    return 1.0 / (1.0 + jnp.exp(-x))


def _log_sigmoid(x):
    return jnp.minimum(x, 0.0) - jnp.log1p(jnp.exp(-jnp.abs(x)))


def _dot(a, b):
    return jnp.dot(a, b, preferred_element_type=F32)


def _lb_kernel(raw_ref, o_ref):
    x = raw_ref[...]
    e = jnp.exp(x - jnp.max(x, axis=0, keepdims=True))
    sm = e / jnp.sum(e, axis=0, keepdims=True)
    depth = x.shape[0]
    acc = sm[0:1, :]
    o_ref[0:1, :] = acc - sm[0:1, :]
    for l in range(1, depth):
        acc = acc + sm[l:l + 1, :]
        o_ref[l:l + 1, :] = acc - sm[0:1, :]


def _lower_bounds(raw):
    return pl.pallas_call(
        _lb_kernel, out_shape=jax.ShapeDtypeStruct(raw.shape, F32), name="hgrn_lower_bounds")(raw)


def _ada_kernel(c_ref, w_ref, b_ref, o_ref):
    c = c_ref[...]
    cs = (c * _sigmoid(c)).astype(BF16)
    o_ref[...] = _dot(cs, w_ref[...].astype(BF16)) + b_ref[...]


def _modulation(c_all, ada_w, ada_b, tn=1024):
    depth, d, n6 = ada_w.shape
    rows = c_all.shape[0]
    return pl.pallas_call(
        _ada_kernel,
        grid=(depth, n6 // tn),
        in_specs=[pl.BlockSpec((rows, d), lambda l, j: (0, 0)),
                  pl.BlockSpec((None, d, tn), lambda l, j: (l, 0, j)),
                  pl.BlockSpec((None, 1, tn), lambda l, j: (l, 0, j))],
        out_specs=pl.BlockSpec((None, rows, tn), lambda l, j: (l, 0, j)),
        out_shape=jax.ShapeDtypeStruct((depth, rows, n6), F32),
        compiler_params=_params("parallel", "parallel"),
        name="ada_modulation",
    )(c_all, ada_w, ada_b.reshape(depth, 1, n6))


def _norm_mod_rows(x_ref, g_ref, sc_ref, sh_ref, h_scr, inv_scr):
    tm = x_ref.shape[0]
    rows = min(tm, 256)

    def stats(r, carry):
        sl = pl.ds(pl.multiple_of(r * rows, rows), rows)
        x = x_ref[sl, :]
        inv_scr[sl, :] = lax.rsqrt(jnp.mean(x * x, axis=-1, keepdims=True) + EPS)
        return carry

    def scale(r, carry):
        sl = pl.ds(pl.multiple_of(r * rows, rows), rows)
        y = x_ref[sl, :] * inv_scr[sl, :] * g_ref[...]
        sc = sc_ref[...] if sc_ref.shape[0] == 1 else sc_ref[sl, :]
        sh = sh_ref[...] if sh_ref.shape[0] == 1 else sh_ref[sl, :]
        h_scr[sl, :] = (y * (1.0 + sc) + sh).astype(BF16)
        return carry

    lax.fori_loop(0, tm // rows, stats, 0)
    lax.fori_loop(0, tm // rows, scale, 0)


def _mod_spec(mod, k, d, tm, tiles_per_stream):
    if mod.ndim == 3:
        return pl.BlockSpec((None, 1, d), lambda i, j: (i // tiles_per_stream, 0, k))
    return pl.BlockSpec((tm, d), lambda i, j: (i, k))


def _col_tile(tn, rows, width, wide):
    t = min(wide if rows <= STREAM_ROWS else tn, width)
    while width % t:
        t //= 2
    return t


def _row_tile(tm, n, mod, rows_per_stream):
    tm = min(tm, n if mod.ndim == 2 else rows_per_stream)
    return tm, max(rows_per_stream // tm, 1)


def _norm_kernel(x_ref, g_ref, sc_ref, sh_ref, wg_ref, bg_ref, h_ref, zg_ref, inv_scr):
    _norm_mod_rows(x_ref, g_ref, sc_ref, sh_ref, h_ref, inv_scr)
    zg_ref[...] = _dot(h_ref[...], wg_ref[...]) + bg_ref[...]


def _norm_gates(x, g, mod, wg, bg, layer, rows_per_stream, tm=512):
    n, d = x.shape
    tm, tps = _row_tile(tm, n, mod, rows_per_stream)
    return pl.pallas_call(
        _norm_kernel,
        grid=(n // tm, 1),
        in_specs=[pl.BlockSpec((tm, d), lambda i, j: (i, 0)),
                  pl.BlockSpec((None, 1, d), lambda i, j: (layer, 0, 0)),
                  _mod_spec(mod, 1, d, tm, tps),
                  _mod_spec(mod, 0, d, tm, tps),
                  pl.BlockSpec((None, d, GATE_PAD), lambda i, j: (layer, 0, 0)),
                  pl.BlockSpec((None, 1, GATE_PAD), lambda i, j: (layer, 0, 0))],
        out_specs=[pl.BlockSpec((tm, d), lambda i, j: (i, 0)),
                   pl.BlockSpec((tm, GATE_PAD), lambda i, j: (i, 0))],
        out_shape=[jax.ShapeDtypeStruct((n, d), BF16),
                   jax.ShapeDtypeStruct((n, GATE_PAD), F32)],
        scratch_shapes=[pltpu.VMEM((tm, 1), F32)],
        compiler_params=_params("parallel", "arbitrary"),
        name="norm_gates",
    )(x, g, mod, mod, wg, bg)


def _mm_in_kernel(h_ref, w_ref, b_ref, z_ref):
    z_ref[...] = _dot(h_ref[...], w_ref[...]) + b_ref[...]


def _mm_in(h, w, b, layer, nz, tm=2048, tn=1024):
    n, d = h.shape
    tm = min(tm, n)
    tn = _col_tile(tn, n, nz, 4096)
    return pl.pallas_call(
        _mm_in_kernel,
        grid=(n // tm, nz // tn),
        in_specs=[pl.BlockSpec((tm, d), lambda i, j: (i, 0)),
                  pl.BlockSpec((None, d, tn), lambda i, j: (layer, 0, j)),
                  pl.BlockSpec((None, 1, tn), lambda i, j: (layer, 0, j))],
        out_specs=pl.BlockSpec((tm, tn), lambda i, j: (i, j)),
        out_shape=jax.ShapeDtypeStruct((n, nz), F32),
        compiler_params=_params("parallel", "arbitrary"),
        name="in_proj",
    )(h, w, b)


def _mlstm_kernel(zq_ref, zk_ref, zv_ref, zo_ref, zg_ref, gt_ref, cw_ref, cb_ref, nrm_ref,
                  conv0_ref, c0_ref, n0_ref, m0_ref,
                  y_ref, c_ref, n_ref, m_ref, hq_scr, hk_scr):
    L = zq_ref.shape[0]
    hd = MA_HEAD_DIM
    heads = range(MA_HEADS)
    sls = [slice(h * hd, (h + 1) * hd) for h in heads]

    @pl.when(pl.program_id(1) == 0)
    def _():
        c_ref[...] = c0_ref[...]
        n_ref[...] = n0_ref[...]
        m_ref[...] = m0_ref[...]
        hq_scr[...] = conv0_ref[0, :, 0:MA_WIDTH]
        hk_scr[...] = conv0_ref[0, :, MA_WIDTH:2 * MA_WIDTH]

    def conv_silu(x_ref, hist_scr, off):
        xs = jnp.concatenate([hist_scr[...], x_ref[...]], axis=0)
        acc = xs * cw_ref[0:1, off:off + MA_WIDTH]
        for j in range(1, CONV_W):
            acc = pltpu.roll(acc, 1, 0) + xs * cw_ref[j:j + 1, off:off + MA_WIDTH]
        acc = acc[CONV_PAD:, :] + cb_ref[:, off:off + MA_WIDTH]
        hist_scr[...] = x_ref[L - CONV_PAD:L, :]
        return acc * _sigmoid(acc)

    q = conv_silu(zq_ref, hq_scr, 0)
    k = conv_silu(zk_ref, hk_scr, MA_WIDTH) * (MA_HEAD_DIM ** -0.5)
    qb = q.astype(BF16)
    kb = k.astype(BF16)
    vb = zv_ref[...].astype(BF16)
    zg = zg_ref[...]
    gates_t = gt_ref[0, 0]

    ti = lax.broadcasted_iota(jnp.int32, (L, L), 0)
    si = lax.broadcasted_iota(jnp.int32, (L, L), 1)
    causal = si <= ti
    tril = causal.astype(BF16)

    def split3(x):
        hi = x.astype(BF16)
        r1 = x - hi.astype(F32)
        mid = r1.astype(BF16)
        return hi, mid, (r1 - mid.astype(F32)).astype(BF16)

    b_cols = sum(_dot(tril, part) for part in split3(_log_sigmoid(zg)))
    b_rows = sum(lax.dot_general(part, tril, _NT, preferred_element_type=F32)
                 for part in split3(_log_sigmoid(gates_t)))

    d, w_inter, w_last, decay, m_t, m_last = [], [], [], [], [], []
    for h in heads:
        ig_col = zg[:, h:h + 1]
        ig_row = gates_t[h:h + 1, :]
        b_col = b_cols[:, MA_HEADS + h:MA_HEADS + h + 1]
        b_row = b_rows[MA_HEADS + h:MA_HEADS + h + 1, :]
        log_d = jnp.where(causal, b_col - b_row + ig_row, NEG)
        log_inter = b_col + m_ref[0, h:h + 1, 0:1]
        m_h = jnp.maximum(log_inter, jnp.max(log_d, axis=1, keepdims=True))
        wi = jnp.exp(log_inter - m_h)
        ml = m_h[L - 1:L, :]
        d.append(jnp.exp(log_d - m_h))
        w_inter.append(wi)
        w_last.append(jnp.exp(b_col[L - 1:L, :] - b_col + ig_col - ml))
        decay.append(wi[L - 1:L, :])
        m_t.append(m_h)
        m_last.append(ml)

    s = [lax.dot_general(qb[:, sl], kb[:, sl], _NT, preferred_element_type=F32) * d[h]
         for h, sl in zip(heads, sls)]
    qc = [_dot(qb[:, sl], c_ref[0, h].astype(BF16)) for h, sl in zip(heads, sls)]
    sv = [_dot(s[h].astype(BF16), vb[:, sl]) for h, sl in zip(heads, sls)]
    kw = [k[:, sl] * w_last[h] for h, sl in zip(heads, sls)]
    upd = [lax.dot_general(kw[h].astype(BF16), vb[:, sl], _TN, preferred_element_type=F32)
           for h, sl in zip(heads, sls)]

    for h, sl in zip(heads, sls):
        n_row = n_ref[0, h:h + 1, :]
        qn = (jnp.sum(s[h], axis=1, keepdims=True)
              + w_inter[h] * jnp.sum(q[:, sl] * n_row, axis=1, keepdims=True))
        denom = jnp.maximum(jnp.abs(qn), jnp.exp(-m_t[h]))
        num = sv[h] + w_inter[h] * qc[h]
        y = num * lax.rsqrt(jnp.mean(num * num, axis=-1, keepdims=True) + EPS * denom * denom)
        y_ref[:, sl] = (y * nrm_ref[:, sl] * _sigmoid(zo_ref[:, sl])).astype(BF16)
        c_ref[0, h] = decay[h] * c_ref[0, h] + upd[h]
        n_ref[0, h:h + 1, :] = decay[h] * n_row + jnp.sum(kw[h], axis=0, keepdims=True)
        m_ref[0, h:h + 1, :] = jnp.broadcast_to(m_last[h], (1, m_ref.shape[2]))


def _mlstm(z, zg, gates_t, conv_w, conv_b, ma_norm, conv0, c0, n0, m0, layer, batch, seq, L):
    nc = seq // L
    w = MA_WIDTH
    blk = lambda col: pl.BlockSpec((L, w), lambda b, c: (b * nc + c, col))
    st4 = pl.BlockSpec((1, MA_HEADS, MA_HEAD_DIM, MA_HEAD_DIM), lambda b, c: (b, 0, 0, 0))
    st3 = pl.BlockSpec((1, MA_HEADS, MA_HEAD_DIM), lambda b, c: (b, 0, 0))
    stm = pl.BlockSpec((1, MA_HEADS, GATE_PAD), lambda b, c: (b, 0, 0))
    return pl.pallas_call(
        _mlstm_kernel,
        grid=(batch, nc),
        in_specs=[blk(0), blk(1), blk(2), blk(3),
                  pl.BlockSpec((L, GATE_PAD), lambda b, c: (b * nc + c, 0)),
                  pl.BlockSpec((1, 1, 2 * MA_HEADS, L), lambda b, c: (b, c, 0, 0)),
                  pl.BlockSpec((None, CONV_W, 2 * w), lambda b, c: (layer, 0, 0)),
                  pl.BlockSpec((None, 1, 2 * w), lambda b, c: (layer, 0, 0)),
                  pl.BlockSpec((None, 1, w), lambda b, c: (layer, 0, 0)),
                  pl.BlockSpec((1, CONV_PAD, 2 * w), lambda b, c: (b, 0, 0)),
                  pl.BlockSpec((None, 1, MA_HEADS, MA_HEAD_DIM, MA_HEAD_DIM),
                               lambda b, c: (layer, b, 0, 0, 0)),
                  pl.BlockSpec((None, 1, MA_HEADS, MA_HEAD_DIM), lambda b, c: (layer, b, 0, 0)),
                  stm],
        out_specs=[pl.BlockSpec((L, w), lambda b, c: (b * nc + c, 0)), st4, st3, stm],
        out_shape=[jax.ShapeDtypeStruct((batch * seq, w), BF16),
                   jax.ShapeDtypeStruct(c0.shape[1:], F32),
                   jax.ShapeDtypeStruct(n0.shape[1:], F32),
                   jax.ShapeDtypeStruct(m0.shape, F32)],
        scratch_shapes=[pltpu.VMEM((CONV_PAD, w), F32), pltpu.VMEM((CONV_PAD, w), F32)],
        compiler_params=_params("parallel", "arbitrary"),
        name="mlstm_scan",
    )(z, z, z, z, zg, gates_t, conv_w, conv_b, ma_norm, conv0, c0, n0, m0)


def _hgrn_kernel(zq_ref, zf_ref, zi_ref, zog_ref, lb_ref, nrm_ref, s0_ref,
                 y_ref, s_ref, st_scr, b_scr, k_scr, q_scr, o_scr):
    L = b_scr.shape[0]
    hd = HB_DIM
    heads = range(HB_HEADS)
    sls = [slice(h * hd, (h + 1) * hd) for h in heads]
    nlast = pl.num_programs(1) - 1

    @pl.when(pl.program_id(1) == 0)
    def _():
        for h in heads:
            st_scr[h] = s0_ref[0, h].T

    nsub = zq_ref.shape[0] // L
    blocks = [slice(u * L, (u + 1) * L) for u in range(nsub)]

    lb = lb_ref[...]
    fb = zf_ref[...]
    e_f = jnp.exp(-jnp.abs(fb))
    r_f = 1.0 / (1.0 + e_f)
    pos = fb >= 0.0
    log_f = jnp.log(lb + (1.0 - lb) * jnp.where(pos, r_f, e_f * r_f))
    k = (1.0 - lb) * jnp.where(pos, e_f * r_f, r_f)
    qr = zq_ref[...]
    q = qr * _sigmoid(qr)
    vb = zi_ref[...].astype(BF16)

    ti = lax.broadcasted_iota(jnp.int32, (L, L), 0)
    si = lax.broadcasted_iota(jnp.int32, (L, L), 1)
    causal = si <= ti
    tril = causal.astype(BF16)
    f_hi = log_f.astype(BF16)
    r1 = log_f - f_hi.astype(F32)
    f_mid = r1.astype(BF16)
    f_lo = (r1 - f_mid.astype(F32)).astype(BF16)
    b = [_dot(tril, f_hi[rw, :]) + _dot(tril, f_mid[rw, :]) + _dot(tril, f_lo[rw, :]) for rw in blocks]

    mid = L // 2
    r = [bu[mid - 1:mid, :] for bu in b]
    b_last = [bu[L - 1:L, :] for bu in b]
    spread = [jnp.max(jnp.maximum(-ru, ru - bl)) for ru, bl in zip(r, b_last)]
    all_safe = functools.reduce(jnp.maximum, spread) <= HGRN_SAFE_EXP

    e_last = [jnp.exp(bl) for bl in b_last]
    qs = [(q[rw, :] * jnp.exp(bu)).astype(BF16) for rw, bu in zip(blocks, b)]
    ks = [(k[rw, :] * jnp.exp(bl - bu)).astype(BF16) for rw, bu, bl in zip(blocks, b, b_last)]
    upd = [[lax.dot_general(vb[rw, sl], ks[u][:, sl], _TN, preferred_element_type=F32) for sl in sls]
           for u, rw in enumerate(blocks)]

    def advance_state(u, intra):
        for h, sl in zip(heads, sls):
            o = lax.dot_general(qs[u][:, sl], st_scr[h].astype(BF16), _NT, preferred_element_type=F32)
            o_scr[blocks[u], sl] = o if intra is None else o + intra[h]
            st_scr[h] = e_last[u][:, sl] * st_scr[h] + upd[u][h]

    def intra_factorised(u):
        rw = blocks[u]
        qt = (q[rw, :] * jnp.exp(b[u] - r[u])).astype(BF16)
        kt = (k[rw, :] * jnp.exp(r[u] - b[u])).astype(BF16)
        a_all = [lax.dot_general(qt[:, sl], kt[:, sl], _NT, preferred_element_type=F32) for sl in sls]
        a_all = [jnp.where(causal, a, 0.0).astype(BF16) for a in a_all]
        return [_dot(a, vb[rw, sl]) for a, sl in zip(a_all, sls)]

    def intra_exact(u):
        rw = blocks[u]
        b_scr[...] = b[u]
        k_scr[...] = k[rw, :]
        q_scr[...] = q[rw, :]
        width = b_scr.shape[1]
        head_of_lane = lax.broadcasted_iota(jnp.int32, (width, GATE_PAD), 0) // hd
        col = lax.broadcasted_iota(jnp.int32, (width, GATE_PAD), 1)
        gather = (head_of_lane == col).astype(BF16)
        head_of_lane_t = lax.broadcasted_iota(jnp.int32, (GATE_PAD, width), 1) // hd
        row = lax.broadcasted_iota(jnp.int32, (GATE_PAD, width), 0)
        scatter = (head_of_lane_t == row).astype(BF16)
        t_idx = lax.broadcasted_iota(jnp.int32, (L, GATE_PAD), 0)

        def body(s, carry):
            b_s = b_scr[pl.ds(s, 1), :]
            k_s = k_scr[pl.ds(s, 1), :]
            v_s = zi_ref[pl.ds(rw.start + s, 1), :]
            e = jnp.exp(jnp.minimum(b_scr[...] - b_s, 0.0)) * q_scr[...] * k_s
            a = jnp.where(t_idx >= s, _dot(e.astype(BF16), gather), 0.0)
            o_scr[rw, :] += _dot(a.astype(BF16), scatter) * v_s
            return carry

        lax.fori_loop(0, L, body, 0)

    @pl.when(all_safe)
    def _():
        intra = [intra_factorised(u) for u in range(nsub)]
        for u in range(nsub):
            advance_state(u, intra[u])

    @pl.when(jnp.logical_not(all_safe))
    def _():
        for u in range(nsub):
            advance_state(u, None)
            safe = spread[u] <= HGRN_SAFE_EXP

            @pl.when(safe)
            def _():
                for sl, o in zip(sls, intra_factorised(u)):
                    o_scr[blocks[u], sl] += o

            @pl.when(jnp.logical_not(safe))
            def _():
                intra_exact(u)

    o_all = [o_scr[:, sl] for sl in sls]
    inv = [lax.rsqrt(jnp.mean(o * o, axis=-1, keepdims=True) + EPS) for o in o_all]
    for sl, o, iv in zip(sls, o_all, inv):
        y_ref[:, sl] = (o * iv * nrm_ref[:, sl] * _sigmoid(zog_ref[:, sl])).astype(BF16)

    @pl.when(pl.program_id(1) == nlast)
    def _():
        for h in heads:
            s_ref[0, h] = st_scr[h].T


def _hgrn(z, lb_all, hb_norm, s0, layer, batch, seq, L):
    rows = min(HB_STEP_ROWS, seq)
    nc = seq // rows
    w = HB_WIDTH
    blk = lambda col: pl.BlockSpec((rows, w), lambda b, c: (b * nc + c, col))
    st = pl.BlockSpec((1, HB_HEADS, HB_DIM, HB_DIM), lambda b, c: (b, 0, 0, 0))
    return pl.pallas_call(
        _hgrn_kernel,
        grid=(batch, nc),
        in_specs=[blk(4), blk(5), blk(6), blk(7),
                  pl.BlockSpec((None, 1, w), lambda b, c: (layer, 0, 0)),
                  pl.BlockSpec((None, 1, w), lambda b, c: (layer, 0, 0)),
                  pl.BlockSpec((None, 1, HB_HEADS, HB_DIM, HB_DIM), lambda b, c: (layer, b, 0, 0, 0))],
        out_specs=[pl.BlockSpec((rows, w), lambda b, c: (b * nc + c, 0)), st],
        out_shape=[jax.ShapeDtypeStruct((batch * seq, w), BF16),
                   jax.ShapeDtypeStruct(s0.shape[1:], F32)],
        scratch_shapes=[pltpu.VMEM((HB_HEADS, HB_DIM, HB_DIM), F32),
                        pltpu.VMEM((L, w), F32), pltpu.VMEM((L, w), F32),
                        pltpu.VMEM((L, w), F32), pltpu.VMEM((rows, w), F32)],
        compiler_params=_params("parallel", "arbitrary"),
        name="hgrn_scan",
    )(z, z, z, z, lb_all, hb_norm, s0)


def _mm_br_kernel(ya_ref, yb_ref, wa_ref, wb_ref, ga_ref, gb_ref, o_ref):
    ta = _dot(ya_ref[...], wa_ref[...])
    tb = _dot(yb_ref[...], wb_ref[...])
    o_ref[...] = (_sigmoid(ga_ref[...]) * ta + _sigmoid(gb_ref[...]) * tb).astype(BF16)


def _mm_branches(ya, yb, wa, wb, z, ga_col, gb_col, layer, tm=1024, tn=512):
    n, kdim = ya.shape
    d = wa.shape[2]
    tm = min(tm, n)
    tn = _col_tile(tn, n, d, 2048)
    ga_blk, gb_blk = ga_col // tn, gb_col // tn
    return pl.pallas_call(
        _mm_br_kernel,
        grid=(n // tm, d // tn),
        in_specs=[pl.BlockSpec((tm, kdim), lambda i, j: (i, 0)),
                  pl.BlockSpec((tm, kdim), lambda i, j: (i, 0)),
                  pl.BlockSpec((None, kdim, tn), lambda i, j: (layer, 0, j)),
                  pl.BlockSpec((None, kdim, tn), lambda i, j: (layer, 0, j)),
                  pl.BlockSpec((tm, tn), lambda i, j: (i, ga_blk + j)),
                  pl.BlockSpec((tm, tn), lambda i, j: (i, gb_blk + j))],
        out_specs=pl.BlockSpec((tm, tn), lambda i, j: (i, j)),
        out_shape=jax.ShapeDtypeStruct((n, d), BF16),
        compiler_params=_params("parallel", "arbitrary"),
        name="branch_merge",
    )(ya, yb, wa, wb, z, z)


def _mm_res_kernel(a_ref, w_ref, x_ref, gt_ref, o_ref):
    o_ref[...] = x_ref[...] + gt_ref[...] * _dot(a_ref[...], w_ref[...])


def _mm_residual(a, w, x, mod, gate_chunk, layer, rows_per_stream, tm=2048, tn=512):
    n, kdim = a.shape
    d = w.shape[2]
    tm, tps = _row_tile(tm, n, mod, rows_per_stream)
    tn = _col_tile(tn, n, d, 2048)
    nj = d // tn
    if mod.ndim == 3:
        gate_spec = pl.BlockSpec((None, 1, tn), lambda i, j: (i // tps, 0, gate_chunk * nj + j))
    else:
        gate_spec = pl.BlockSpec((tm, tn), lambda i, j: (i, gate_chunk * nj + j))
    return pl.pallas_call(
        _mm_res_kernel,
        grid=(n // tm, nj),
        in_specs=[pl.BlockSpec((tm, kdim), lambda i, j: (i, 0)),
                  pl.BlockSpec((None, kdim, tn), lambda i, j: (layer, 0, j)),
                  pl.BlockSpec((tm, tn), lambda i, j: (i, j)),
                  gate_spec],
        out_specs=pl.BlockSpec((tm, tn), lambda i, j: (i, j)),
        out_shape=jax.ShapeDtypeStruct((n, d), F32),
        compiler_params=_params("parallel", "arbitrary"),
        name="out_proj_residual",
    )(a, w, x, mod)


def _ffn_kernel(x_ref, g_ref, sc_ref, sh_ref, gt_ref, wu_ref, wd_ref, fg_ref, o_ref, h_scr, inv_scr,
                *, final):
    f = pl.program_id(1)

    @pl.when(f == 0)
    def _():
        _norm_mod_rows(x_ref, g_ref, sc_ref, sh_ref, h_scr, inv_scr)
        o_ref[...] = jnp.zeros_like(o_ref)

    u = jnp.square(jnp.maximum(_dot(h_scr[...], wu_ref[...]), 0.0))
    o_ref[...] += _dot(u.astype(BF16), wd_ref[...])

    @pl.when(f == pl.num_programs(1) - 1)
    def _():
        tm = x_ref.shape[0]
        rows = min(tm, 256)

        def body(r, carry):
            sl = pl.ds(pl.multiple_of(r * rows, rows), rows)
            gt = gt_ref[...] if gt_ref.shape[0] == 1 else gt_ref[sl, :]
            xn = x_ref[sl, :] + gt * o_ref[sl, :]
            if final:
                xn = xn * lax.rsqrt(jnp.mean(xn * xn, axis=-1, keepdims=True) + EPS) * fg_ref[...]
            o_ref[sl, :] = xn
            return carry

        lax.fori_loop(0, tm // rows, body, 0)


def _ffn(x, g, mod, wu, wd, final_g, final, layer, rows_per_stream, tm=1024, tf=1024):
    n, d = x.shape
    dff = wu.shape[2]
    tm, tps = _row_tile(tm, n, mod, rows_per_stream)
    tf = _col_tile(tf, n, dff, 2048)
    return pl.pallas_call(
        functools.partial(_ffn_kernel, final=final),
        grid=(n // tm, dff // tf),
        in_specs=[pl.BlockSpec((tm, d), lambda i, j: (i, 0), pipeline_mode=pl.Buffered(1)),
                  pl.BlockSpec((None, 1, d), lambda i, j: (layer, 0, 0)),
                  _mod_spec(mod, 4, d, tm, tps),
                  _mod_spec(mod, 3, d, tm, tps),
                  _mod_spec(mod, 5, d, tm, tps),
                  pl.BlockSpec((None, d, tf), lambda i, j: (layer, 0, j)),
                  pl.BlockSpec((None, tf, d), lambda i, j: (layer, j, 0)),
                  pl.BlockSpec((1, d), lambda i, j: (0, 0))],
        out_specs=pl.BlockSpec((tm, d), lambda i, j: (i, 0)),
        out_shape=jax.ShapeDtypeStruct((n, d), F32),
        scratch_shapes=[pltpu.VMEM((tm, d), BF16), pltpu.VMEM((tm, 1), F32)],
        compiler_params=_params("parallel", "arbitrary"),
        name="ffn_residual",
    )(x, g, mod, mod, mod, wu, wd, final_g)


def _trunk(x, mod_all, per_token_mod, conv_c, st_c, st_n, st_m, st_s, lb_all, wts):
    batch, seq, d = x.shape
    depth = conv_c.shape[0]
    L = min(MA_CHUNK, seq)
    Lh = min(HB_CHUNK, seq)
    nc = seq // L
    n = batch * seq
    nz = wts["b_in"].shape[2]
    keep = CONV_W - 1
    xf = x.reshape(n, d)
    bufs, cms, nvs, mrs, sms = [], [], [], [], []
    for l in range(depth):
        mod = mod_all[l]
        if per_token_mod:
            mod = jnp.repeat(mod, seq, axis=0)
        else:
            mod = mod[:, None, :]
        h, zg = _norm_gates(xf, wts["norm1_g"], mod, wts["w_gate"], wts["b_gate"], l, seq)
        z = _mm_in(h, wts["w_in"], wts["b_in"], l, nz)
        gates_t = jnp.swapaxes(zg[:, :2 * MA_HEADS].reshape(batch, nc, L, 2 * MA_HEADS), 2, 3)
        conv0 = jnp.pad(conv_c[l], ((0, 0), (CONV_PAD - keep, 0), (0, 0)))
        m0 = jnp.broadcast_to(st_m[l][:, :, None], (batch, MA_HEADS, GATE_PAD))
        ya, c1, n1, m1 = _mlstm(z, zg, gates_t, wts["conv_w"], wts["conv_b"], wts["ma_norm"],
                                conv0, st_c, st_n, m0, l, batch, seq, L)
        yb, s1 = _hgrn(z, lb_all, wts["hb_norm"], st_s, l, batch, seq, Lh)
        merged = _mm_branches(ya, yb, wts["w_br_a"], wts["w_br_b"], z,
                              4 * MA_WIDTH + 4 * HB_WIDTH, 4 * MA_WIDTH + 4 * HB_WIDTH + d, l)
        xf = _mm_residual(merged, wts["w_o"], xf, mod, 2, l, seq)
        xf = _ffn(xf, wts["norm2_g"], mod, wts["w_up"], wts["w_down"], wts["final_g"],
                  l == depth - 1, l, seq)
        bufs.append(z.reshape(batch, seq, nz)[:, seq - keep:, :2 * MA_WIDTH])
        cms.append(c1)
        nvs.append(n1)
        mrs.append(m1[:, :, 0])
        sms.append(s1)
    return (xf.reshape(batch, seq, d), jnp.stack(bufs), jnp.stack(cms), jnp.stack(nvs),
            jnp.stack(mrs), jnp.stack(sms))


def kernel(x_prompt, x_sample, cache_conv, state_mlstm_C, state_mlstm_n, state_mlstm_m, state_hgrn,
           c_prompt, c_sample, ada_w, ada_b, norm1_g, norm2_g, w_in, b_in, conv_w, conv_b, ma_norm,
           hgrn_lb_raw, hb_norm, w_br_a, w_br_b, w_o, w_up, w_down, final_g):
    depth, d, n_in = w_in.shape
    bp, bs = x_prompt.shape[0], x_sample.shape[0]
    n_main = n_in - 2 * MA_HEADS
    gate_pad = GATE_PAD - 2 * MA_HEADS

    lb_all = _lower_bounds(hgrn_lb_raw).reshape(depth, 1, HB_WIDTH)
    c_all = jnp.concatenate([c_prompt, c_sample], axis=0)
    c_rows = -(-c_all.shape[0] // 8) * 8
    c_all = jnp.pad(c_all, ((0, c_rows - c_all.shape[0]), (0, 0)))
    mod_all = _modulation(c_all, ada_w, ada_b)

    wts = {
        "norm1_g": norm1_g.reshape(depth, 1, d),
        "norm2_g": norm2_g.reshape(depth, 1, d),
        "w_in": w_in.astype(BF16),
        "b_in": b_in[:, :n_main].reshape(depth, 1, n_main),
        "w_gate": jnp.pad(w_in[:, :, n_main:], ((0, 0), (0, 0), (0, gate_pad))).astype(BF16),
        "b_gate": jnp.pad(b_in[:, n_main:], ((0, 0), (0, gate_pad))).reshape(depth, 1, GATE_PAD),
        "conv_w": conv_w,
        "conv_b": conv_b.reshape(depth, 1, -1),
        "ma_norm": ma_norm.reshape(depth, 1, -1),
        "hb_norm": hb_norm.reshape(depth, 1, -1),
        "w_br_a": w_br_a.astype(BF16),
        "w_br_b": w_br_b.astype(BF16),
        "w_o": w_o.astype(BF16),
        "w_up": w_up.astype(BF16),
        "w_down": w_down.astype(BF16),
        "final_g": final_g.reshape(1, d),
    }

    f32 = jnp.float32
    z_conv = jnp.zeros((depth, bp, CONV_W - 1, 2 * MA_WIDTH), f32)
    z_c = jnp.zeros((depth, bp, MA_HEADS, MA_HEAD_DIM, MA_HEAD_DIM), f32)
    z_n = jnp.zeros((depth, bp, MA_HEADS, MA_HEAD_DIM), f32)
    z_m = jnp.zeros((depth, bp, MA_HEADS), f32)
    z_s = jnp.zeros((depth, bp, HB_HEADS, HB_DIM, HB_DIM), f32)

    out_p = _trunk(x_prompt, mod_all[:, :bp], False, z_conv, z_c, z_n, z_m, z_s, lb_all, wts)
    out_s = _trunk(x_sample, mod_all[:, bp:bp + bs], True, cache_conv, state_mlstm_C, state_mlstm_n,
                   state_mlstm_m, state_hgrn, lb_all, wts)
    y_p, conv_p, c_p, n_p, m_p, s_p = out_p
    y_s, conv_s, c_s, n_s, m_s, s_s = out_s
    return (y_p, y_s, conv_p, c_p, n_p, m_p, s_p, conv_s, c_s, n_s, m_s, s_s)
```

```python
import functools

import jax
import jax.numpy as jnp
from jax import lax
from jax.experimental import pallas as pl
from jax.experimental.pallas import tpu as pltpu

EPS = 1e-6
NEG = -1e30
MA_CHUNK = 256
HB_CHUNK = 64
HB_STEP_ROWS = 256
MA_HEADS = 8
MA_HEAD_DIM = 256
MA_WIDTH = MA_HEADS * MA_HEAD_DIM
CONV_W = 4
HB_HEADS = 16
HB_DIM = 128
HB_WIDTH = HB_HEADS * HB_DIM
GATE_PAD = 128
CONV_PAD = 8
STREAM_ROWS = 256
HGRN_SAFE_LOG2 = 115.0
VMEM_LIMIT = 62 * 1024 * 1024

F32 = jnp.float32
BF16 = jnp.bfloat16

_NT = (((1,), (1,)), ((), ()))
_TN = (((0,), (0,)), ((), ()))


def _params(*sem):
    return pltpu.CompilerParams(dimension_semantics=sem, vmem_limit_bytes=VMEM_LIMIT)


def _sigmoid(x):
    return 1.0 / (1.0 + jnp.exp(-x))


def _log_sigmoid(x):
    return jnp.minimum(x, 0.0) - jnp.log1p(jnp.exp(-jnp.abs(x)))


def _dot(a, b):
    return jnp.dot(a, b, preferred_element_type=F32)


def _lb_kernel(raw_ref, o_ref):
    x = raw_ref[...]
    e = jnp.exp(x - jnp.max(x, axis=0, keepdims=True))
    sm = e / jnp.sum(e, axis=0, keepdims=True)
    depth = x.shape[0]
    acc = sm[0:1, :]
    o_ref[0:1, :] = acc - sm[0:1, :]
    for l in range(1, depth):
        acc = acc + sm[l:l + 1, :]
        o_ref[l:l + 1, :] = acc - sm[0:1, :]


def _lower_bounds(raw):
    return pl.pallas_call(
        _lb_kernel, out_shape=jax.ShapeDtypeStruct(raw.shape, F32), name="hgrn_lower_bounds")(raw)


def _ada_kernel(c_ref, w_ref, b_ref, o_ref):
    c = c_ref[...]
    cs = (c * _sigmoid(c)).astype(BF16)
    o_ref[...] = _dot(cs, w_ref[...].astype(BF16)) + b_ref[...]


def _modulation(c_all, ada_w, ada_b, tn=1024):
    depth, d, n6 = ada_w.shape
    rows = c_all.shape[0]
    return pl.pallas_call(
        _ada_kernel,
        grid=(depth, n6 // tn),
        in_specs=[pl.BlockSpec((rows, d), lambda l, j: (0, 0)),
                  pl.BlockSpec((None, d, tn), lambda l, j: (l, 0, j)),
                  pl.BlockSpec((None, 1, tn), lambda l, j: (l, 0, j))],
        out_specs=pl.BlockSpec((None, rows, tn), lambda l, j: (l, 0, j)),
        out_shape=jax.ShapeDtypeStruct((depth, rows, n6), F32),
        compiler_params=_params("parallel", "parallel"),
        name="ada_modulation",
    )(c_all, ada_w, ada_b.reshape(depth, 1, n6))


def _norm_mod_rows(x_ref, g_ref, sc_ref, sh_ref, h_scr, inv_scr):
    tm = x_ref.shape[0]
    rows = min(tm, 256)

    def stats(r, carry):
        sl = pl.ds(pl.multiple_of(r * rows, rows), rows)
        x = x_ref[sl, :]
        inv_scr[sl, :] = lax.rsqrt(jnp.mean(x * x, axis=-1, keepdims=True) + EPS)
        return carry

    def scale(r, carry):
        sl = pl.ds(pl.multiple_of(r * rows, rows), rows)
        y = x_ref[sl, :] * inv_scr[sl, :] * g_ref[...]
        sc = sc_ref[...] if sc_ref.shape[0] == 1 else sc_ref[sl, :]
        sh = sh_ref[...] if sh_ref.shape[0] == 1 else sh_ref[sl, :]
        h_scr[sl, :] = (y * (1.0 + sc) + sh).astype(BF16)
        return carry

    lax.fori_loop(0, tm // rows, stats, 0)
    lax.fori_loop(0, tm // rows, scale, 0)


def _mod_spec(mod, k, d, tm, tiles_per_stream):
    if mod.ndim == 3:
        return pl.BlockSpec((None, 1, d), lambda i, j: (i // tiles_per_stream, 0, k))
    return pl.BlockSpec((tm, d), lambda i, j: (i, k))


def _col_tile(tn, rows, width, wide):
    t = min(wide if rows <= STREAM_ROWS else tn, width)
    while width % t:
        t //= 2
    return t


def _row_tile(tm, n, mod, rows_per_stream):
    tm = min(tm, n if mod.ndim == 2 else rows_per_stream)
    return tm, max(rows_per_stream // tm, 1)


def _norm_kernel(x_ref, g_ref, sc_ref, sh_ref, wg_ref, bg_ref, h_ref, zg_ref, inv_scr):
    _norm_mod_rows(x_ref, g_ref, sc_ref, sh_ref, h_ref, inv_scr)
    zg_ref[...] = _dot(h_ref[...], wg_ref[...]) + bg_ref[...]


def _norm_gates(x, g, mod, wg, bg, layer, rows_per_stream, tm=1024):
    n, d = x.shape
    tm, tps = _row_tile(tm, n, mod, rows_per_stream)
    return pl.pallas_call(
        _norm_kernel,
        grid=(n // tm, 1),
        in_specs=[pl.BlockSpec((tm, d), lambda i, j: (i, 0)),
                  pl.BlockSpec((None, 1, d), lambda i, j: (layer, 0, 0)),
                  _mod_spec(mod, 1, d, tm, tps),
                  _mod_spec(mod, 0, d, tm, tps),
                  pl.BlockSpec((None, d, GATE_PAD), lambda i, j: (layer, 0, 0)),
                  pl.BlockSpec((None, 1, GATE_PAD), lambda i, j: (layer, 0, 0))],
        out_specs=[pl.BlockSpec((tm, d), lambda i, j: (i, 0)),
                   pl.BlockSpec((tm, GATE_PAD), lambda i, j: (i, 0))],
        out_shape=[jax.ShapeDtypeStruct((n, d), BF16),
                   jax.ShapeDtypeStruct((n, GATE_PAD), F32)],
        scratch_shapes=[pltpu.VMEM((tm, 1), F32)],
        compiler_params=_params("parallel", "arbitrary"),
        name="norm_gates",
    )(x, g, mod, mod, wg, bg)


def _mm_in_kernel(h_ref, w_ref, b_ref, z_ref):
    z_ref[...] = _dot(h_ref[...], w_ref[...]) + b_ref[...]


def _mm_in(h, w, b, layer, nz, tm=2048, tn=1024):
    n, d = h.shape
    tm = min(tm, n)
    tn = _col_tile(tn, n, nz, 4096)
    return pl.pallas_call(
        _mm_in_kernel,
        grid=(n // tm, nz // tn),
        in_specs=[pl.BlockSpec((tm, d), lambda i, j: (i, 0)),
                  pl.BlockSpec((None, d, tn), lambda i, j: (layer, 0, j)),
                  pl.BlockSpec((None, 1, tn), lambda i, j: (layer, 0, j))],
        out_specs=pl.BlockSpec((tm, tn), lambda i, j: (i, j)),
        out_shape=jax.ShapeDtypeStruct((n, nz), F32),
        compiler_params=_params("parallel", "arbitrary"),
        name="in_proj",
    )(h, w, b)


def _mlstm_kernel(zq_ref, zk_ref, zv_ref, zo_ref, zg_ref, gt_ref, cw_ref, cb_ref, nrm_ref,
                  conv0_ref, c0_ref, n0_ref, m0_ref,
                  y_ref, c_ref, n_ref, m_ref, hq_scr, hk_scr):
    L = zq_ref.shape[0]
    hd = MA_HEAD_DIM
    heads = range(MA_HEADS)
    sls = [slice(h * hd, (h + 1) * hd) for h in heads]

    @pl.when(pl.program_id(1) == 0)
    def _():
        c_ref[...] = c0_ref[...]
        n_ref[...] = n0_ref[...]
        m_ref[...] = m0_ref[...]
        hq_scr[...] = conv0_ref[0, :, 0:MA_WIDTH]
        hk_scr[...] = conv0_ref[0, :, MA_WIDTH:2 * MA_WIDTH]

    def conv_silu(x_ref, hist_scr, off):
        xs = jnp.concatenate([hist_scr[...], x_ref[...]], axis=0)
        acc = xs * cw_ref[0:1, off:off + MA_WIDTH]
        for j in range(1, CONV_W):
            acc = pltpu.roll(acc, 1, 0) + xs * cw_ref[j:j + 1, off:off + MA_WIDTH]
        acc = acc[CONV_PAD:, :] + cb_ref[:, off:off + MA_WIDTH]
        hist_scr[...] = x_ref[L - CONV_PAD:L, :]
        return acc * _sigmoid(acc)

    q = conv_silu(zq_ref, hq_scr, 0)
    k = conv_silu(zk_ref, hk_scr, MA_WIDTH) * (MA_HEAD_DIM ** -0.5)
    qb = q.astype(BF16)
    kb = k.astype(BF16)
    vb = zv_ref[...].astype(BF16)
    zg = zg_ref[...]
    gates_t = gt_ref[0, 0]

    ti = lax.broadcasted_iota(jnp.int32, (L, L), 0)
    si = lax.broadcasted_iota(jnp.int32, (L, L), 1)
    causal = si <= ti
    tril = causal.astype(BF16)

    def split3(x):
        hi = x.astype(BF16)
        r1 = x - hi.astype(F32)
        mid = r1.astype(BF16)
        return hi, mid, (r1 - mid.astype(F32)).astype(BF16)

    b_cols = sum(_dot(tril, part) for part in split3(_log_sigmoid(zg)))
    b_rows = sum(lax.dot_general(part, tril, _NT, preferred_element_type=F32)
                 for part in split3(_log_sigmoid(gates_t)))

    d, w_inter, w_last, decay, m_t, m_last = [], [], [], [], [], []
    for h in heads:
        ig_col = zg[:, h:h + 1]
        ig_row = gates_t[h:h + 1, :]
        b_col = b_cols[:, MA_HEADS + h:MA_HEADS + h + 1]
        b_row = b_rows[MA_HEADS + h:MA_HEADS + h + 1, :]
        log_d = jnp.where(causal, b_col - b_row + ig_row, NEG)
        log_inter = b_col + m_ref[0, h:h + 1, 0:1]
        m_h = jnp.maximum(log_inter, jnp.max(log_d, axis=1, keepdims=True))
        wi = jnp.exp(log_inter - m_h)
        ml = m_h[L - 1:L, :]
        d.append(jnp.exp(log_d - m_h))
        w_inter.append(wi)
        w_last.append(jnp.exp(b_col[L - 1:L, :] - b_col + ig_col - ml))
        decay.append(wi[L - 1:L, :])
        m_t.append(m_h)
        m_last.append(ml)

    s = [lax.dot_general(qb[:, sl], kb[:, sl], _NT, preferred_element_type=F32) * d[h]
         for h, sl in zip(heads, sls)]
    qc = [_dot(qb[:, sl], c_ref[0, h].astype(BF16)) for h, sl in zip(heads, sls)]
    sv = [_dot(s[h].astype(BF16), vb[:, sl]) for h, sl in zip(heads, sls)]
    kw = [k[:, sl] * w_last[h] for h, sl in zip(heads, sls)]
    upd = [lax.dot_general(kw[h].astype(BF16), vb[:, sl], _TN, preferred_element_type=F32)
           for h, sl in zip(heads, sls)]

    for h, sl in zip(heads, sls):
        n_row = n_ref[0, h:h + 1, :]
        qn = (jnp.sum(s[h], axis=1, keepdims=True)
              + w_inter[h] * jnp.sum(q[:, sl] * n_row, axis=1, keepdims=True))
        denom = jnp.maximum(jnp.abs(qn), jnp.exp(-m_t[h]))
        num = sv[h] + w_inter[h] * qc[h]
        y = num * lax.rsqrt(jnp.mean(num * num, axis=-1, keepdims=True) + EPS * denom * denom)
        y_ref[:, sl] = (y * nrm_ref[:, sl] * _sigmoid(zo_ref[:, sl])).astype(BF16)
        c_ref[0, h] = decay[h] * c_ref[0, h] + upd[h]
        n_ref[0, h:h + 1, :] = decay[h] * n_row + jnp.sum(kw[h], axis=0, keepdims=True)
        m_ref[0, h:h + 1, :] = jnp.broadcast_to(m_last[h], (1, m_ref.shape[2]))


def _mlstm(z, zg, gates_t, conv_w, conv_b, ma_norm, conv0, c0, n0, m0, layer, batch, seq, L):
    nc = seq // L
    w = MA_WIDTH
    blk = lambda col: pl.BlockSpec((L, w), lambda b, c: (b * nc + c, col))
    st4 = pl.BlockSpec((1, MA_HEADS, MA_HEAD_DIM, MA_HEAD_DIM), lambda b, c: (b, 0, 0, 0))
    st3 = pl.BlockSpec((1, MA_HEADS, MA_HEAD_DIM), lambda b, c: (b, 0, 0))
    stm = pl.BlockSpec((1, MA_HEADS, GATE_PAD), lambda b, c: (b, 0, 0))
    return pl.pallas_call(
        _mlstm_kernel,
        grid=(batch, nc),
        in_specs=[blk(0), blk(1), blk(2), blk(3),
                  pl.BlockSpec((L, GATE_PAD), lambda b, c: (b * nc + c, 0)),
                  pl.BlockSpec((1, 1, 2 * MA_HEADS, L), lambda b, c: (b, c, 0, 0)),
                  pl.BlockSpec((None, CONV_W, 2 * w), lambda b, c: (layer, 0, 0)),
                  pl.BlockSpec((None, 1, 2 * w), lambda b, c: (layer, 0, 0)),
                  pl.BlockSpec((None, 1, w), lambda b, c: (layer, 0, 0)),
                  pl.BlockSpec((1, CONV_PAD, 2 * w), lambda b, c: (b, 0, 0)),
                  pl.BlockSpec((None, 1, MA_HEADS, MA_HEAD_DIM, MA_HEAD_DIM),
                               lambda b, c: (layer, b, 0, 0, 0)),
                  pl.BlockSpec((None, 1, MA_HEADS, MA_HEAD_DIM), lambda b, c: (layer, b, 0, 0)),
                  stm],
        out_specs=[pl.BlockSpec((L, w), lambda b, c: (b * nc + c, 0)), st4, st3, stm],
        out_shape=[jax.ShapeDtypeStruct((batch * seq, w), BF16),
                   jax.ShapeDtypeStruct(c0.shape[1:], F32),
                   jax.ShapeDtypeStruct(n0.shape[1:], F32),
                   jax.ShapeDtypeStruct(m0.shape, F32)],
        scratch_shapes=[pltpu.VMEM((CONV_PAD, w), F32), pltpu.VMEM((CONV_PAD, w), F32)],
        compiler_params=_params("parallel", "arbitrary"),
        name="mlstm_scan",
    )(z, z, z, z, zg, gates_t, conv_w, conv_b, ma_norm, conv0, c0, n0, m0)


def _hgrn_kernel(zq_ref, zf_ref, zi_ref, zog_ref, lb_ref, nrm_ref, s0_ref,
                 y_ref, s_ref, st_scr, b_scr, k_scr, q_scr, o_scr):
    L = b_scr.shape[0]
    hd = HB_DIM
    heads = range(HB_HEADS)
    sls = [slice(h * hd, (h + 1) * hd) for h in heads]
    nlast = pl.num_programs(1) - 1

    @pl.when(pl.program_id(1) == 0)
    def _():
        for h in heads:
            st_scr[h] = s0_ref[0, h].T

    nsub = zq_ref.shape[0] // L
    blocks = [slice(u * L, (u + 1) * L) for u in range(nsub)]

    lb = lb_ref[...]
    fb = zf_ref[...]
    e_f = jnp.exp(-jnp.abs(fb))
    r_f = 1.0 / (1.0 + e_f)
    pos = fb >= 0.0
    log_f = jnp.log2(lb + (1.0 - lb) * jnp.where(pos, r_f, e_f * r_f))
    k = (1.0 - lb) * jnp.where(pos, e_f * r_f, r_f)
    qr = zq_ref[...]
    q = qr * _sigmoid(qr)
    vb = zi_ref[...].astype(BF16)

    ti = lax.broadcasted_iota(jnp.int32, (L, L), 0)
    si = lax.broadcasted_iota(jnp.int32, (L, L), 1)
    causal = si <= ti
    tril = causal.astype(BF16)
    f_hi = log_f.astype(BF16)
    r1 = log_f - f_hi.astype(F32)
    f_mid = r1.astype(BF16)
    f_lo = (r1 - f_mid.astype(F32)).astype(BF16)
    b = [_dot(tril, f_hi[rw, :]) + _dot(tril, f_mid[rw, :]) + _dot(tril, f_lo[rw, :]) for rw in blocks]

    mid = L // 2
    r = [bu[mid - 1:mid, :] for bu in b]
    b_last = [bu[L - 1:L, :] for bu in b]
    spread = [jnp.max(jnp.maximum(-ru, ru - bl)) for ru, bl in zip(r, b_last)]
    all_safe = functools.reduce(jnp.maximum, spread) <= HGRN_SAFE_LOG2

    e_last = [jnp.exp2(bl) for bl in b_last]
    qs = [(q[rw, :] * jnp.exp2(bu)).astype(BF16) for rw, bu in zip(blocks, b)]
    ks = [(k[rw, :] * jnp.exp2(bl - bu)).astype(BF16) for rw, bu, bl in zip(blocks, b, b_last)]
    upd = [[lax.dot_general(vb[rw, sl], ks[u][:, sl], _TN, preferred_element_type=F32) for sl in sls]
           for u, rw in enumerate(blocks)]

    def advance_state(u, intra):
        for h, sl in zip(heads, sls):
            o = lax.dot_general(qs[u][:, sl], st_scr[h].astype(BF16), _NT, preferred_element_type=F32)
            o_scr[blocks[u], sl] = o if intra is None else o + intra[h]
            st_scr[h] = e_last[u][:, sl] * st_scr[h] + upd[u][h]

    def intra_factorised(u):
        rw = blocks[u]
        qt = (q[rw, :] * jnp.exp2(b[u] - r[u])).astype(BF16)
        kt = (k[rw, :] * jnp.exp2(r[u] - b[u])).astype(BF16)
        a_all = [lax.dot_general(qt[:, sl], kt[:, sl], _NT, preferred_element_type=F32) for sl in sls]
        a_all = [jnp.where(causal, a, 0.0).astype(BF16) for a in a_all]
        return [_dot(a, vb[rw, sl]) for a, sl in zip(a_all, sls)]

    def intra_exact(u):
        rw = blocks[u]
        b_scr[...] = b[u]
        k_scr[...] = k[rw, :]
        q_scr[...] = q[rw, :]
        width = b_scr.shape[1]
        head_of_lane = lax.broadcasted_iota(jnp.int32, (width, GATE_PAD), 0) // hd
        col = lax.broadcasted_iota(jnp.int32, (width, GATE_PAD), 1)
        gather = (head_of_lane == col).astype(BF16)
        head_of_lane_t = lax.broadcasted_iota(jnp.int32, (GATE_PAD, width), 1) // hd
        row = lax.broadcasted_iota(jnp.int32, (GATE_PAD, width), 0)
        scatter = (head_of_lane_t == row).astype(BF16)
        t_idx = lax.broadcasted_iota(jnp.int32, (L, GATE_PAD), 0)

        def body(s, carry):
            b_s = b_scr[pl.ds(s, 1), :]
            k_s = k_scr[pl.ds(s, 1), :]
            v_s = zi_ref[pl.ds(rw.start + s, 1), :]
            e = jnp.exp2(jnp.minimum(b_scr[...] - b_s, 0.0)) * q_scr[...] * k_s
            a = jnp.where(t_idx >= s, _dot(e.astype(BF16), gather), 0.0)
            o_scr[rw, :] += _dot(a.astype(BF16), scatter) * v_s
            return carry

        lax.fori_loop(0, L, body, 0)

    @pl.when(all_safe)
    def _():
        intra = [intra_factorised(u) for u in range(nsub)]
        for u in range(nsub):
            advance_state(u, intra[u])

    @pl.when(jnp.logical_not(all_safe))
    def _():
        for u in range(nsub):
            advance_state(u, None)
            safe = spread[u] <= HGRN_SAFE_LOG2

            @pl.when(safe)
            def _():
                for sl, o in zip(sls, intra_factorised(u)):
                    o_scr[blocks[u], sl] += o

            @pl.when(jnp.logical_not(safe))
            def _():
                intra_exact(u)

    o_all = [o_scr[:, sl] for sl in sls]
    inv = [lax.rsqrt(jnp.mean(o * o, axis=-1, keepdims=True) + EPS) for o in o_all]
    for sl, o, iv in zip(sls, o_all, inv):
        y_ref[:, sl] = (o * iv * nrm_ref[:, sl] * _sigmoid(zog_ref[:, sl])).astype(BF16)

    @pl.when(pl.program_id(1) == nlast)
    def _():
        for h in heads:
            s_ref[0, h] = st_scr[h].T


def _hgrn(z, lb_all, hb_norm, s0, layer, batch, seq, L):
    rows = min(HB_STEP_ROWS, seq)
    nc = seq // rows
    w = HB_WIDTH
    blk = lambda col: pl.BlockSpec((rows, w), lambda b, c: (b * nc + c, col))
    st = pl.BlockSpec((1, HB_HEADS, HB_DIM, HB_DIM), lambda b, c: (b, 0, 0, 0))
    return pl.pallas_call(
        _hgrn_kernel,
        grid=(batch, nc),
        in_specs=[blk(4), blk(5), blk(6), blk(7),
                  pl.BlockSpec((None, 1, w), lambda b, c: (layer, 0, 0)),
                  pl.BlockSpec((None, 1, w), lambda b, c: (layer, 0, 0)),
                  pl.BlockSpec((None, 1, HB_HEADS, HB_DIM, HB_DIM), lambda b, c: (layer, b, 0, 0, 0))],
        out_specs=[pl.BlockSpec((rows, w), lambda b, c: (b * nc + c, 0)), st],
        out_shape=[jax.ShapeDtypeStruct((batch * seq, w), BF16),
                   jax.ShapeDtypeStruct(s0.shape[1:], F32)],
        scratch_shapes=[pltpu.VMEM((HB_HEADS, HB_DIM, HB_DIM), F32),
                        pltpu.VMEM((L, w), F32), pltpu.VMEM((L, w), F32),
                        pltpu.VMEM((L, w), F32), pltpu.VMEM((rows, w), F32)],
        compiler_params=_params("parallel", "arbitrary"),
        name="hgrn_scan",
    )(z, z, z, z, lb_all, hb_norm, s0)


def _mm_br_kernel(ya_ref, yb_ref, wa_ref, wb_ref, ga_ref, gb_ref, o_ref):
    ta = _dot(ya_ref[...], wa_ref[...])
    tb = _dot(yb_ref[...], wb_ref[...])
    o_ref[...] = (_sigmoid(ga_ref[...]) * ta + _sigmoid(gb_ref[...]) * tb).astype(BF16)


def _mm_branches(ya, yb, wa, wb, z, ga_col, gb_col, layer, tm=1024, tn=1024):
    n, kdim = ya.shape
    d = wa.shape[2]
    tm = min(tm, n)
    tn = _col_tile(tn, n, d, 2048)
    ga_blk, gb_blk = ga_col // tn, gb_col // tn
    return pl.pallas_call(
        _mm_br_kernel,
        grid=(n // tm, d // tn),
        in_specs=[pl.BlockSpec((tm, kdim), lambda i, j: (i, 0)),
                  pl.BlockSpec((tm, kdim), lambda i, j: (i, 0)),
                  pl.BlockSpec((None, kdim, tn), lambda i, j: (layer, 0, j)),
                  pl.BlockSpec((None, kdim, tn), lambda i, j: (layer, 0, j)),
                  pl.BlockSpec((tm, tn), lambda i, j: (i, ga_blk + j)),
                  pl.BlockSpec((tm, tn), lambda i, j: (i, gb_blk + j))],
        out_specs=pl.BlockSpec((tm, tn), lambda i, j: (i, j)),
        out_shape=jax.ShapeDtypeStruct((n, d), BF16),
        compiler_params=_params("parallel", "arbitrary"),
        name="branch_merge",
    )(ya, yb, wa, wb, z, z)


def _mm_res_kernel(a_ref, w_ref, x_ref, gt_ref, o_ref):
    o_ref[...] = x_ref[...] + gt_ref[...] * _dot(a_ref[...], w_ref[...])


def _mm_residual(a, w, x, mod, gate_chunk, layer, rows_per_stream, tm=2048, tn=512):
    n, kdim = a.shape
    d = w.shape[2]
    tm, tps = _row_tile(tm, n, mod, rows_per_stream)
    tn = _col_tile(tn, n, d, 2048)
    nj = d // tn
    if mod.ndim == 3:
        gate_spec = pl.BlockSpec((None, 1, tn), lambda i, j: (i // tps, 0, gate_chunk * nj + j))
    else:
        gate_spec = pl.BlockSpec((tm, tn), lambda i, j: (i, gate_chunk * nj + j))
    return pl.pallas_call(
        _mm_res_kernel,
        grid=(n // tm, nj),
        in_specs=[pl.BlockSpec((tm, kdim), lambda i, j: (i, 0)),
                  pl.BlockSpec((None, kdim, tn), lambda i, j: (layer, 0, j)),
                  pl.BlockSpec((tm, tn), lambda i, j: (i, j)),
                  gate_spec],
        out_specs=pl.BlockSpec((tm, tn), lambda i, j: (i, j)),
        out_shape=jax.ShapeDtypeStruct((n, d), F32),
        compiler_params=_params("parallel", "arbitrary"),
        name="out_proj_residual",
    )(a, w, x, mod)


def _ffn_kernel(x_ref, g_ref, sc_ref, sh_ref, gt_ref, wu_ref, wd_ref, fg_ref, o_ref, h_scr, inv_scr,
                *, final):
    f = pl.program_id(1)

    @pl.when(f == 0)
    def _():
        _norm_mod_rows(x_ref, g_ref, sc_ref, sh_ref, h_scr, inv_scr)
        o_ref[...] = jnp.zeros_like(o_ref)

    u = jnp.square(jnp.maximum(_dot(h_scr[...], wu_ref[...]), 0.0))
    o_ref[...] += _dot(u.astype(BF16), wd_ref[...])

    @pl.when(f == pl.num_programs(1) - 1)
    def _():
        tm = x_ref.shape[0]
        rows = min(tm, 256)

        def body(r, carry):
            sl = pl.ds(pl.multiple_of(r * rows, rows), rows)
            gt = gt_ref[...] if gt_ref.shape[0] == 1 else gt_ref[sl, :]
            xn = x_ref[sl, :] + gt * o_ref[sl, :]
            if final:
                xn = xn * lax.rsqrt(jnp.mean(xn * xn, axis=-1, keepdims=True) + EPS) * fg_ref[...]
            o_ref[sl, :] = xn
            return carry

        lax.fori_loop(0, tm // rows, body, 0)


def _ffn(x, g, mod, wu, wd, final_g, final, layer, rows_per_stream, tm=1024, tf=1024):
    n, d = x.shape
    dff = wu.shape[2]
    tm, tps = _row_tile(tm, n, mod, rows_per_stream)
    tf = _col_tile(tf, n, dff, 2048)
    return pl.pallas_call(
        functools.partial(_ffn_kernel, final=final),
        grid=(n // tm, dff // tf),
        in_specs=[pl.BlockSpec((tm, d), lambda i, j: (i, 0)),
                  pl.BlockSpec((None, 1, d), lambda i, j: (layer, 0, 0)),
                  _mod_spec(mod, 4, d, tm, tps),
                  _mod_spec(mod, 3, d, tm, tps),
                  _mod_spec(mod, 5, d, tm, tps),
                  pl.BlockSpec((None, d, tf), lambda i, j: (layer, 0, j)),
                  pl.BlockSpec((None, tf, d), lambda i, j: (layer, j, 0)),
                  pl.BlockSpec((1, d), lambda i, j: (0, 0))],
        out_specs=pl.BlockSpec((tm, d), lambda i, j: (i, 0)),
        out_shape=jax.ShapeDtypeStruct((n, d), F32),
        scratch_shapes=[pltpu.VMEM((tm, d), BF16), pltpu.VMEM((tm, 1), F32)],
        compiler_params=_params("parallel", "arbitrary"),
        name="ffn_residual",
    )(x, g, mod, mod, mod, wu, wd, final_g)


def _trunk(x, mod_all, per_token_mod, conv_c, st_c, st_n, st_m, st_s, lb_all, wts):
    batch, seq, d = x.shape
    depth = conv_c.shape[0]
    L = min(MA_CHUNK, seq)
    Lh = min(HB_CHUNK, seq)
    nc = seq // L
    n = batch * seq
    nz = wts["b_in"].shape[2]
    keep = CONV_W - 1
    xf = x.reshape(n, d)
    bufs, cms, nvs, mrs, sms = [], [], [], [], []
    for l in range(depth):
        mod = mod_all[l]
        if per_token_mod:
            mod = jnp.repeat(mod, seq, axis=0)
        else:
            mod = mod[:, None, :]
        h, zg = _norm_gates(xf, wts["norm1_g"], mod, wts["w_gate"], wts["b_gate"], l, seq)
        z = _mm_in(h, wts["w_in"], wts["b_in"], l, nz)
        gates_t = jnp.swapaxes(zg[:, :2 * MA_HEADS].reshape(batch, nc, L, 2 * MA_HEADS), 2, 3)
        conv0 = jnp.pad(conv_c[l], ((0, 0), (CONV_PAD - keep, 0), (0, 0)))
        m0 = jnp.broadcast_to(st_m[l][:, :, None], (batch, MA_HEADS, GATE_PAD))
        ya, c1, n1, m1 = _mlstm(z, zg, gates_t, wts["conv_w"], wts["conv_b"], wts["ma_norm"],
                                conv0, st_c, st_n, m0, l, batch, seq, L)
        yb, s1 = _hgrn(z, lb_all, wts["hb_norm"], st_s, l, batch, seq, Lh)
        merged = _mm_branches(ya, yb, wts["w_br_a"], wts["w_br_b"], z,
                              4 * MA_WIDTH + 4 * HB_WIDTH, 4 * MA_WIDTH + 4 * HB_WIDTH + d, l)
        xf = _mm_residual(merged, wts["w_o"], xf, mod, 2, l, seq)
        xf = _ffn(xf, wts["norm2_g"], mod, wts["w_up"], wts["w_down"], wts["final_g"],
                  l == depth - 1, l, seq)
        bufs.append(z.reshape(batch, seq, nz)[:, seq - keep:, :2 * MA_WIDTH])
        cms.append(c1)
        nvs.append(n1)
        mrs.append(m1[:, :, 0])
        sms.append(s1)
    return (xf.reshape(batch, seq, d), jnp.stack(bufs), jnp.stack(cms), jnp.stack(nvs),
            jnp.stack(mrs), jnp.stack(sms))


def kernel(x_prompt, x_sample, cache_conv, state_mlstm_C, state_mlstm_n, state_mlstm_m, state_hgrn,
           c_prompt, c_sample, ada_w, ada_b, norm1_g, norm2_g, w_in, b_in, conv_w, conv_b, ma_norm,
           hgrn_lb_raw, hb_norm, w_br_a, w_br_b, w_o, w_up, w_down, final_g):
    depth, d, n_in = w_in.shape
    bp, bs = x_prompt.shape[0], x_sample.shape[0]
    n_main = n_in - 2 * MA_HEADS
    gate_pad = GATE_PAD - 2 * MA_HEADS

    lb_all = _lower_bounds(hgrn_lb_raw).reshape(depth, 1, HB_WIDTH)
    c_all = jnp.concatenate([c_prompt, c_sample], axis=0)
    c_rows = -(-c_all.shape[0] // 8) * 8
    c_all = jnp.pad(c_all, ((0, c_rows - c_all.shape[0]), (0, 0)))
    mod_all = _modulation(c_all, ada_w, ada_b)

    wts = {
        "norm1_g": norm1_g.reshape(depth, 1, d),
        "norm2_g": norm2_g.reshape(depth, 1, d),
        "w_in": w_in.astype(BF16),
        "b_in": b_in[:, :n_main].reshape(depth, 1, n_main),
        "w_gate": jnp.pad(w_in[:, :, n_main:], ((0, 0), (0, 0), (0, gate_pad))).astype(BF16),
        "b_gate": jnp.pad(b_in[:, n_main:], ((0, 0), (0, gate_pad))).reshape(depth, 1, GATE_PAD),
        "conv_w": conv_w,
        "conv_b": conv_b.reshape(depth, 1, -1),
        "ma_norm": ma_norm.reshape(depth, 1, -1),
        "hb_norm": hb_norm.reshape(depth, 1, -1),
        "w_br_a": w_br_a.astype(BF16),
        "w_br_b": w_br_b.astype(BF16),
        "w_o": w_o.astype(BF16),
        "w_up": w_up.astype(BF16),
        "w_down": w_down.astype(BF16),
        "final_g": final_g.reshape(1, d),
    }

    f32 = jnp.float32
    z_conv = jnp.zeros((depth, bp, CONV_W - 1, 2 * MA_WIDTH), f32)
    z_c = jnp.zeros((depth, bp, MA_HEADS, MA_HEAD_DIM, MA_HEAD_DIM), f32)
    z_n = jnp.zeros((depth, bp, MA_HEADS, MA_HEAD_DIM), f32)
    z_m = jnp.zeros((depth, bp, MA_HEADS), f32)
    z_s = jnp.zeros((depth, bp, HB_HEADS, HB_DIM, HB_DIM), f32)

    out_p = _trunk(x_prompt, mod_all[:, :bp], False, z_conv, z_c, z_n, z_m, z_s, lb_all, wts)
    out_s = _trunk(x_sample, mod_all[:, bp:bp + bs], True, cache_conv, state_mlstm_C, state_mlstm_n,
                   state_mlstm_m, state_hgrn, lb_all, wts)
    y_p, conv_p, c_p, n_p, m_p, s_p = out_p
    y_s, conv_s, c_s, n_s, m_s, s_s = out_s
    return (y_p, y_s, conv_p, c_p, n_p, m_p, s_p, conv_s, c_s, n_s, m_s, s_s)
```

```python
import functools

import jax
import jax.numpy as jnp
from jax import lax
from jax.experimental import pallas as pl
from jax.experimental.pallas import tpu as pltpu

EPS = 1e-6
NEG = -1e30
MA_CHUNK = 256
HB_CHUNK = 64
HB_STEP_ROWS = 256
MA_HEADS = 8
MA_HEAD_DIM = 256
MA_WIDTH = MA_HEADS * MA_HEAD_DIM
CONV_W = 4
HB_HEADS = 16
HB_DIM = 128
HB_WIDTH = HB_HEADS * HB_DIM
GATE_PAD = 128
CONV_PAD = 8
BF16_ROWS = 16
STREAM_ROWS = 256
HGRN_SAFE_LOG2 = 115.0
VMEM_LIMIT = 62 * 1024 * 1024

F32 = jnp.float32
BF16 = jnp.bfloat16

_NT = (((1,), (1,)), ((), ()))
_TN = (((0,), (0,)), ((), ()))


def _params(*sem):
    return pltpu.CompilerParams(dimension_semantics=sem, vmem_limit_bytes=VMEM_LIMIT)


def _sigmoid(x):
    return 1.0 / (1.0 + jnp.exp(-x))


def _log_sigmoid(x):
    return jnp.minimum(x, 0.0) - jnp.log1p(jnp.exp(-jnp.abs(x)))


def _dot(a, b):
    return jnp.dot(a, b, preferred_element_type=F32)


def _lb_kernel(raw_ref, o_ref):
    x = raw_ref[...]
    e = jnp.exp(x - jnp.max(x, axis=0, keepdims=True))
    sm = e / jnp.sum(e, axis=0, keepdims=True)
    depth = x.shape[0]
    acc = sm[0:1, :]
    o_ref[0:1, :] = acc - sm[0:1, :]
    for l in range(1, depth):
        acc = acc + sm[l:l + 1, :]
        o_ref[l:l + 1, :] = acc - sm[0:1, :]


def _lower_bounds(raw):
    return pl.pallas_call(
        _lb_kernel, out_shape=jax.ShapeDtypeStruct(raw.shape, F32), name="hgrn_lower_bounds")(raw)


def _ada_kernel(c_ref, w_ref, b_ref, o_ref):
    c = c_ref[...]
    cs = (c * _sigmoid(c)).astype(BF16)
    o_ref[...] = _dot(cs, w_ref[...].astype(BF16)) + b_ref[...]


def _modulation(c_all, ada_w, ada_b, tn=1024):
    depth, d, n6 = ada_w.shape
    rows = c_all.shape[0]
    return pl.pallas_call(
        _ada_kernel,
        grid=(depth, n6 // tn),
        in_specs=[pl.BlockSpec((rows, d), lambda l, j: (0, 0)),
                  pl.BlockSpec((None, d, tn), lambda l, j: (l, 0, j)),
                  pl.BlockSpec((None, 1, tn), lambda l, j: (l, 0, j))],
        out_specs=pl.BlockSpec((None, rows, tn), lambda l, j: (l, 0, j)),
        out_shape=jax.ShapeDtypeStruct((depth, rows, n6), F32),
        compiler_params=_params("parallel", "parallel"),
        name="ada_modulation",
    )(c_all, ada_w, ada_b.reshape(depth, 1, n6))


def _norm_mod_rows(x_ref, g_ref, sc_ref, sh_ref, h_scr, inv_scr):
    tm = x_ref.shape[0]
    rows = min(tm, 256)

    def stats(r, carry):
        sl = pl.ds(pl.multiple_of(r * rows, rows), rows)
        x = x_ref[sl, :]
        inv_scr[sl, :] = lax.rsqrt(jnp.mean(x * x, axis=-1, keepdims=True) + EPS)
        return carry

    def scale(r, carry):
        sl = pl.ds(pl.multiple_of(r * rows, rows), rows)
        y = x_ref[sl, :] * inv_scr[sl, :] * g_ref[...]
        sc = sc_ref[...] if sc_ref.shape[0] == 1 else sc_ref[sl, :]
        sh = sh_ref[...] if sh_ref.shape[0] == 1 else sh_ref[sl, :]
        h_scr[sl, :] = (y * (1.0 + sc) + sh).astype(BF16)
        return carry

    lax.fori_loop(0, tm // rows, stats, 0)
    lax.fori_loop(0, tm // rows, scale, 0)


def _mod_spec(mod, k, d, tm, tiles_per_stream):
    if mod.ndim == 3:
        return pl.BlockSpec((None, 1, d), lambda i, j: (i // tiles_per_stream, 0, k))
    return pl.BlockSpec((tm, d), lambda i, j: (i, k))


def _col_tile(tn, rows, width, wide):
    t = min(wide if rows <= STREAM_ROWS else tn, width)
    while width % t:
        t //= 2
    return t


def _row_tile(tm, n, mod, rows_per_stream):
    tm = min(tm, n if mod.ndim == 2 else rows_per_stream)
    return tm, max(rows_per_stream // tm, 1)


def _norm_kernel(x_ref, g_ref, sc_ref, sh_ref, wg_ref, bg_ref, h_ref, zg_ref, inv_scr):
    _norm_mod_rows(x_ref, g_ref, sc_ref, sh_ref, h_ref, inv_scr)
    zg_ref[...] = _dot(h_ref[...], wg_ref[...]) + bg_ref[...]


def _norm_gates(x, g, mod, wg, bg, layer, rows_per_stream, tm=1024):
    n, d = x.shape
    tm, tps = _row_tile(tm, n, mod, rows_per_stream)
    return pl.pallas_call(
        _norm_kernel,
        grid=(n // tm, 1),
        in_specs=[pl.BlockSpec((tm, d), lambda i, j: (i, 0)),
                  pl.BlockSpec((None, 1, d), lambda i, j: (layer, 0, 0)),
                  _mod_spec(mod, 1, d, tm, tps),
                  _mod_spec(mod, 0, d, tm, tps),
                  pl.BlockSpec((None, d, GATE_PAD), lambda i, j: (layer, 0, 0)),
                  pl.BlockSpec((None, 1, GATE_PAD), lambda i, j: (layer, 0, 0))],
        out_specs=[pl.BlockSpec((tm, d), lambda i, j: (i, 0)),
                   pl.BlockSpec((tm, GATE_PAD), lambda i, j: (i, 0))],
        out_shape=[jax.ShapeDtypeStruct((n, d), BF16),
                   jax.ShapeDtypeStruct((n, GATE_PAD), F32)],
        scratch_shapes=[pltpu.VMEM((tm, 1), F32)],
        compiler_params=_params("parallel", "arbitrary"),
        name="norm_gates",
    )(x, g, mod, mod, wg, bg)


def _mm_in_kernel(h_ref, w_ref, b_ref, *rest):
    n_side = (len(rest) - 1) // 2
    z_ref = rest[n_side]
    z_ref[...] = _dot(h_ref[...], w_ref[...]) + b_ref[...]
    for f32_ref, bf16_ref in zip(rest[:n_side], rest[n_side + 1:]):
        bf16_ref[...] = f32_ref[...].astype(BF16)


def _mm_in(h, w, b, layer, nz, side=(), tm=2048, tn=1024):
    n, d = h.shape
    tm = min(tm, n)
    tn = _col_tile(tn, n, nz, 4096)
    nj = nz // tn
    steps = (n // tm) * nj
    side_in, side_out, side_shape = [], [], []
    for s in side:
        rows, cols = s.shape[1:]
        slab = BF16_ROWS
        while slab * steps < rows:
            slab *= 2
        last = rows // slab - 1
        side_in.append(pl.BlockSpec(
            (None, slab, cols), lambda i, j, last=last: (layer, jnp.minimum(i * nj + j, last), 0)))
        side_out.append(pl.BlockSpec(
            (slab, cols), lambda i, j, last=last: (jnp.minimum(i * nj + j, last), 0)))
        side_shape.append(jax.ShapeDtypeStruct((rows, cols), BF16))
    out = pl.pallas_call(
        _mm_in_kernel,
        grid=(n // tm, nj),
        in_specs=[pl.BlockSpec((tm, d), lambda i, j: (i, 0)),
                  pl.BlockSpec((None, d, tn), lambda i, j: (layer, 0, j)),
                  pl.BlockSpec((None, 1, tn), lambda i, j: (layer, 0, j))] + side_in,
        out_specs=[pl.BlockSpec((tm, tn), lambda i, j: (i, j))] + side_out,
        out_shape=[jax.ShapeDtypeStruct((n, nz), F32)] + side_shape,
        compiler_params=_params("arbitrary", "arbitrary"),
        name="in_proj",
    )(h, w, b, *side)
    return out[0], out[1:]


def _mlstm_kernel(zq_ref, zk_ref, zv_ref, zo_ref, zg_ref, gt_ref, cw_ref, cb_ref, nrm_ref,
                  conv0_ref, c0_ref, n0_ref, m0_ref,
                  y_ref, c_ref, n_ref, m_ref, hq_scr, hk_scr):
    L = zq_ref.shape[0]
    hd = MA_HEAD_DIM
    heads = range(MA_HEADS)
    sls = [slice(h * hd, (h + 1) * hd) for h in heads]

    @pl.when(pl.program_id(1) == 0)
    def _():
        c_ref[...] = c0_ref[...]
        n_ref[...] = n0_ref[...]
        m_ref[...] = m0_ref[...]
        hq_scr[...] = conv0_ref[0, :, 0:MA_WIDTH]
        hk_scr[...] = conv0_ref[0, :, MA_WIDTH:2 * MA_WIDTH]

    def conv_silu(x_ref, hist_scr, off):
        xs = jnp.concatenate([hist_scr[...], x_ref[...]], axis=0)
        acc = xs * cw_ref[0:1, off:off + MA_WIDTH]
        for j in range(1, CONV_W):
            acc = pltpu.roll(acc, 1, 0) + xs * cw_ref[j:j + 1, off:off + MA_WIDTH]
        acc = acc[CONV_PAD:, :] + cb_ref[:, off:off + MA_WIDTH]
        hist_scr[...] = x_ref[L - CONV_PAD:L, :]
        return acc * _sigmoid(acc)

    q = conv_silu(zq_ref, hq_scr, 0)
    k = conv_silu(zk_ref, hk_scr, MA_WIDTH) * (MA_HEAD_DIM ** -0.5)
    qb = q.astype(BF16)
    kb = k.astype(BF16)
    vb = zv_ref[...].astype(BF16)
    zg = zg_ref[...]
    gates_t = gt_ref[0, 0]

    ti = lax.broadcasted_iota(jnp.int32, (L, L), 0)
    si = lax.broadcasted_iota(jnp.int32, (L, L), 1)
    causal = si <= ti
    tril = causal.astype(BF16)

    def split3(x):
        hi = x.astype(BF16)
        r1 = x - hi.astype(F32)
        mid = r1.astype(BF16)
        return hi, mid, (r1 - mid.astype(F32)).astype(BF16)

    b_cols = sum(_dot(tril, part) for part in split3(_log_sigmoid(zg)))
    b_rows = sum(lax.dot_general(part, tril, _NT, preferred_element_type=F32)
                 for part in split3(_log_sigmoid(gates_t)))

    d, w_inter, w_last, decay, m_t, m_last = [], [], [], [], [], []
    for h in heads:
        ig_col = zg[:, h:h + 1]
        ig_row = gates_t[h:h + 1, :]
        b_col = b_cols[:, MA_HEADS + h:MA_HEADS + h + 1]
        b_row = b_rows[MA_HEADS + h:MA_HEADS + h + 1, :]
        log_d = jnp.where(causal, b_col - b_row + ig_row, NEG)
        log_inter = b_col + m_ref[0, h:h + 1, 0:1]
        m_h = jnp.maximum(log_inter, jnp.max(log_d, axis=1, keepdims=True))
        wi = jnp.exp(log_inter - m_h)
        ml = m_h[L - 1:L, :]
        d.append(jnp.exp(log_d - m_h))
        w_inter.append(wi)
        w_last.append(jnp.exp(b_col[L - 1:L, :] - b_col + ig_col - ml))
        decay.append(wi[L - 1:L, :])
        m_t.append(m_h)
        m_last.append(ml)

    s = [lax.dot_general(qb[:, sl], kb[:, sl], _NT, preferred_element_type=F32) * d[h]
         for h, sl in zip(heads, sls)]
    qc = [_dot(qb[:, sl], c_ref[0, h].astype(BF16)) for h, sl in zip(heads, sls)]
    sv = [_dot(s[h].astype(BF16), vb[:, sl]) for h, sl in zip(heads, sls)]
    kw = [k[:, sl] * w_last[h] for h, sl in zip(heads, sls)]
    upd = [lax.dot_general(kw[h].astype(BF16), vb[:, sl], _TN, preferred_element_type=F32)
           for h, sl in zip(heads, sls)]

    for h, sl in zip(heads, sls):
        n_row = n_ref[0, h:h + 1, :]
        qn = (jnp.sum(s[h], axis=1, keepdims=True)
              + w_inter[h] * jnp.sum(q[:, sl] * n_row, axis=1, keepdims=True))
        denom = jnp.maximum(jnp.abs(qn), jnp.exp(-m_t[h]))
        num = sv[h] + w_inter[h] * qc[h]
        y = num * lax.rsqrt(jnp.mean(num * num, axis=-1, keepdims=True) + EPS * denom * denom)
        y_ref[:, sl] = (y * nrm_ref[:, sl] * _sigmoid(zo_ref[:, sl])).astype(BF16)
        c_ref[0, h] = decay[h] * c_ref[0, h] + upd[h]
        n_ref[0, h:h + 1, :] = decay[h] * n_row + jnp.sum(kw[h], axis=0, keepdims=True)
        m_ref[0, h:h + 1, :] = jnp.broadcast_to(m_last[h], (1, m_ref.shape[2]))


def _mlstm(z, zg, gates_t, conv_w, conv_b, ma_norm, conv0, c0, n0, m0, layer, batch, seq, L):
    nc = seq // L
    w = MA_WIDTH
    blk = lambda col: pl.BlockSpec((L, w), lambda b, c: (b * nc + c, col))
    st4 = pl.BlockSpec((1, MA_HEADS, MA_HEAD_DIM, MA_HEAD_DIM), lambda b, c: (b, 0, 0, 0))
    st3 = pl.BlockSpec((1, MA_HEADS, MA_HEAD_DIM), lambda b, c: (b, 0, 0))
    stm = pl.BlockSpec((1, MA_HEADS, GATE_PAD), lambda b, c: (b, 0, 0))
    return pl.pallas_call(
        _mlstm_kernel,
        grid=(batch, nc),
        in_specs=[blk(0), blk(1), blk(2), blk(3),
                  pl.BlockSpec((L, GATE_PAD), lambda b, c: (b * nc + c, 0)),
                  pl.BlockSpec((1, 1, 2 * MA_HEADS, L), lambda b, c: (b, c, 0, 0)),
                  pl.BlockSpec((None, CONV_W, 2 * w), lambda b, c: (layer, 0, 0)),
                  pl.BlockSpec((None, 1, 2 * w), lambda b, c: (layer, 0, 0)),
                  pl.BlockSpec((None, 1, w), lambda b, c: (layer, 0, 0)),
                  pl.BlockSpec((1, CONV_PAD, 2 * w), lambda b, c: (b, 0, 0)),
                  pl.BlockSpec((None, 1, MA_HEADS, MA_HEAD_DIM, MA_HEAD_DIM),
                               lambda b, c: (layer, b, 0, 0, 0)),
                  pl.BlockSpec((None, 1, MA_HEADS, MA_HEAD_DIM), lambda b, c: (layer, b, 0, 0)),
                  stm],
        out_specs=[pl.BlockSpec((L, w), lambda b, c: (b * nc + c, 0)), st4, st3, stm],
        out_shape=[jax.ShapeDtypeStruct((batch * seq, w), BF16),
                   jax.ShapeDtypeStruct(c0.shape[1:], F32),
                   jax.ShapeDtypeStruct(n0.shape[1:], F32),
                   jax.ShapeDtypeStruct(m0.shape, F32)],
        scratch_shapes=[pltpu.VMEM((CONV_PAD, w), F32), pltpu.VMEM((CONV_PAD, w), F32)],
        compiler_params=_params("parallel", "arbitrary"),
        name="mlstm_scan",
    )(z, z, z, z, zg, gates_t, conv_w, conv_b, ma_norm, conv0, c0, n0, m0)


def _hgrn_kernel(zq_ref, zf_ref, zi_ref, zog_ref, lb_ref, nrm_ref, s0_ref,
                 y_ref, s_ref, st_scr, b_scr, k_scr, q_scr, o_scr):
    L = b_scr.shape[0]
    hd = HB_DIM
    heads = range(HB_HEADS)
    sls = [slice(h * hd, (h + 1) * hd) for h in heads]
    nlast = pl.num_programs(1) - 1

    @pl.when(pl.program_id(1) == 0)
    def _():
        for h in heads:
            st_scr[h] = s0_ref[0, h].T

    nsub = zq_ref.shape[0] // L
    blocks = [slice(u * L, (u + 1) * L) for u in range(nsub)]

    lb = lb_ref[...]
    fb = zf_ref[...]
    e_f = jnp.exp(-jnp.abs(fb))
    r_f = 1.0 / (1.0 + e_f)
    pos = fb >= 0.0
    log_f = jnp.log2(lb + (1.0 - lb) * jnp.where(pos, r_f, e_f * r_f))
    k = (1.0 - lb) * jnp.where(pos, e_f * r_f, r_f)
    qr = zq_ref[...]
    q = qr * _sigmoid(qr)
    vb = zi_ref[...].astype(BF16)

    ti = lax.broadcasted_iota(jnp.int32, (L, L), 0)
    si = lax.broadcasted_iota(jnp.int32, (L, L), 1)
    causal = si <= ti
    tril = causal.astype(BF16)
    f_hi = log_f.astype(BF16)
    r1 = log_f - f_hi.astype(F32)
    f_mid = r1.astype(BF16)
    f_lo = (r1 - f_mid.astype(F32)).astype(BF16)
    b = [_dot(tril, f_hi[rw, :]) + _dot(tril, f_mid[rw, :]) + _dot(tril, f_lo[rw, :]) for rw in blocks]

    mid = L // 2
    r = [bu[mid - 1:mid, :] for bu in b]
    b_last = [bu[L - 1:L, :] for bu in b]
    spread = [jnp.max(jnp.maximum(-ru, ru - bl)) for ru, bl in zip(r, b_last)]
    all_safe = functools.reduce(jnp.maximum, spread) <= HGRN_SAFE_LOG2

    e_last = [jnp.exp2(bl) for bl in b_last]
    qs = [(q[rw, :] * jnp.exp2(bu)).astype(BF16) for rw, bu in zip(blocks, b)]
    ks = [(k[rw, :] * jnp.exp2(bl - bu)).astype(BF16) for rw, bu, bl in zip(blocks, b, b_last)]
    upd = [[lax.dot_general(vb[rw, sl], ks[u][:, sl], _TN, preferred_element_type=F32) for sl in sls]
           for u, rw in enumerate(blocks)]

    def advance_state(u, intra):
        for h, sl in zip(heads, sls):
            o = lax.dot_general(qs[u][:, sl], st_scr[h].astype(BF16), _NT, preferred_element_type=F32)
            o_scr[blocks[u], sl] = o if intra is None else o + intra[h]
            st_scr[h] = e_last[u][:, sl] * st_scr[h] + upd[u][h]

    def intra_factorised(u):
        rw = blocks[u]
        qt = (q[rw, :] * jnp.exp2(b[u] - r[u])).astype(BF16)
        kt = (k[rw, :] * jnp.exp2(r[u] - b[u])).astype(BF16)
        a_all = [lax.dot_general(qt[:, sl], kt[:, sl], _NT, preferred_element_type=F32) for sl in sls]
        a_all = [jnp.where(causal, a, 0.0).astype(BF16) for a in a_all]
        return [_dot(a, vb[rw, sl]) for a, sl in zip(a_all, sls)]

    def intra_exact(u):
        rw = blocks[u]
        b_scr[...] = b[u]
        k_scr[...] = k[rw, :]
        q_scr[...] = q[rw, :]
        width = b_scr.shape[1]
        head_of_lane = lax.broadcasted_iota(jnp.int32, (width, GATE_PAD), 0) // hd
        col = lax.broadcasted_iota(jnp.int32, (width, GATE_PAD), 1)
        gather = (head_of_lane == col).astype(BF16)
        head_of_lane_t = lax.broadcasted_iota(jnp.int32, (GATE_PAD, width), 1) // hd
        row = lax.broadcasted_iota(jnp.int32, (GATE_PAD, width), 0)
        scatter = (head_of_lane_t == row).astype(BF16)
        t_idx = lax.broadcasted_iota(jnp.int32, (L, GATE_PAD), 0)

        def body(s, carry):
            b_s = b_scr[pl.ds(s, 1), :]
            k_s = k_scr[pl.ds(s, 1), :]
            v_s = zi_ref[pl.ds(rw.start + s, 1), :]
            e = jnp.exp2(jnp.minimum(b_scr[...] - b_s, 0.0)) * q_scr[...] * k_s
            a = jnp.where(t_idx >= s, _dot(e.astype(BF16), gather), 0.0)
            o_scr[rw, :] += _dot(a.astype(BF16), scatter) * v_s
            return carry

        lax.fori_loop(0, L, body, 0)

    @pl.when(all_safe)
    def _():
        intra = [intra_factorised(u) for u in range(nsub)]
        for u in range(nsub):
            advance_state(u, intra[u])

    @pl.when(jnp.logical_not(all_safe))
    def _():
        for u in range(nsub):
            advance_state(u, None)
            safe = spread[u] <= HGRN_SAFE_LOG2

            @pl.when(safe)
            def _():
                for sl, o in zip(sls, intra_factorised(u)):
                    o_scr[blocks[u], sl] += o

            @pl.when(jnp.logical_not(safe))
            def _():
                intra_exact(u)

    o_all = [o_scr[:, sl] for sl in sls]
    inv = [lax.rsqrt(jnp.mean(o * o, axis=-1, keepdims=True) + EPS) for o in o_all]
    for sl, o, iv in zip(sls, o_all, inv):
        y_ref[:, sl] = (o * iv * nrm_ref[:, sl] * _sigmoid(zog_ref[:, sl])).astype(BF16)

    @pl.when(pl.program_id(1) == nlast)
    def _():
        for h in heads:
            s_ref[0, h] = st_scr[h].T


def _hgrn(z, lb_all, hb_norm, s0, layer, batch, seq, L):
    rows = min(HB_STEP_ROWS, seq)
    nc = seq // rows
    w = HB_WIDTH
    blk = lambda col: pl.BlockSpec((rows, w), lambda b, c: (b * nc + c, col))
    st = pl.BlockSpec((1, HB_HEADS, HB_DIM, HB_DIM), lambda b, c: (b, 0, 0, 0))
    return pl.pallas_call(
        _hgrn_kernel,
        grid=(batch, nc),
        in_specs=[blk(4), blk(5), blk(6), blk(7),
                  pl.BlockSpec((None, 1, w), lambda b, c: (layer, 0, 0)),
                  pl.BlockSpec((None, 1, w), lambda b, c: (layer, 0, 0)),
                  pl.BlockSpec((None, 1, HB_HEADS, HB_DIM, HB_DIM), lambda b, c: (layer, b, 0, 0, 0))],
        out_specs=[pl.BlockSpec((rows, w), lambda b, c: (b * nc + c, 0)), st],
        out_shape=[jax.ShapeDtypeStruct((batch * seq, w), BF16),
                   jax.ShapeDtypeStruct(s0.shape[1:], F32)],
        scratch_shapes=[pltpu.VMEM((HB_HEADS, HB_DIM, HB_DIM), F32),
                        pltpu.VMEM((L, w), F32), pltpu.VMEM((L, w), F32),
                        pltpu.VMEM((L, w), F32), pltpu.VMEM((rows, w), F32)],
        compiler_params=_params("parallel", "arbitrary"),
        name="hgrn_scan",
    )(z, z, z, z, lb_all, hb_norm, s0)


def _mm_br_kernel(ya_ref, yb_ref, wa_ref, wb_ref, ga_ref, gb_ref, o_ref):
    ta = _dot(ya_ref[...], wa_ref[...])
    tb = _dot(yb_ref[...], wb_ref[...])
    o_ref[...] = (_sigmoid(ga_ref[...]) * ta + _sigmoid(gb_ref[...]) * tb).astype(BF16)


def _mm_branches(ya, yb, wa, wb, z, ga_col, gb_col, layer, tm=1024, tn=1024):
    n, kdim = ya.shape
    d = wa.shape[1]
    tm = min(tm, n)
    tn = _col_tile(tn, n, d, 2048)
    ga_blk, gb_blk = ga_col // tn, gb_col // tn
    return pl.pallas_call(
        _mm_br_kernel,
        grid=(n // tm, d // tn),
        in_specs=[pl.BlockSpec((tm, kdim), lambda i, j: (i, 0)),
                  pl.BlockSpec((tm, kdim), lambda i, j: (i, 0)),
                  pl.BlockSpec((kdim, tn), lambda i, j: (0, j)),
                  pl.BlockSpec((kdim, tn), lambda i, j: (0, j)),
                  pl.BlockSpec((tm, tn), lambda i, j: (i, ga_blk + j)),
                  pl.BlockSpec((tm, tn), lambda i, j: (i, gb_blk + j))],
        out_specs=pl.BlockSpec((tm, tn), lambda i, j: (i, j)),
        out_shape=jax.ShapeDtypeStruct((n, d), BF16),
        compiler_params=_params("parallel", "arbitrary"),
        name="branch_merge",
    )(ya, yb, wa, wb, z, z)


def _mm_res_kernel(a_ref, w_ref, x_ref, gt_ref, o_ref):
    o_ref[...] = x_ref[...] + gt_ref[...] * _dot(a_ref[...], w_ref[...])


def _mm_residual(a, w, x, mod, gate_chunk, layer, rows_per_stream, tm=2048, tn=512):
    n, kdim = a.shape
    d = w.shape[1]
    tm, tps = _row_tile(tm, n, mod, rows_per_stream)
    tn = _col_tile(tn, n, d, 2048)
    nj = d // tn
    if mod.ndim == 3:
        gate_spec = pl.BlockSpec((None, 1, tn), lambda i, j: (i // tps, 0, gate_chunk * nj + j))
    else:
        gate_spec = pl.BlockSpec((tm, tn), lambda i, j: (i, gate_chunk * nj + j))
    return pl.pallas_call(
        _mm_res_kernel,
        grid=(n // tm, nj),
        in_specs=[pl.BlockSpec((tm, kdim), lambda i, j: (i, 0)),
                  pl.BlockSpec((kdim, tn), lambda i, j: (0, j)),
                  pl.BlockSpec((tm, tn), lambda i, j: (i, j)),
                  gate_spec],
        out_specs=pl.BlockSpec((tm, tn), lambda i, j: (i, j)),
        out_shape=jax.ShapeDtypeStruct((n, d), F32),
        compiler_params=_params("parallel", "arbitrary"),
        name="out_proj_residual",
    )(a, w, x, mod)


def _ffn_kernel(x_ref, g_ref, sc_ref, sh_ref, gt_ref, wu_ref, wd_ref, fg_ref, o_ref, h_scr, inv_scr,
                *, final):
    f = pl.program_id(1)

    @pl.when(f == 0)
    def _():
        _norm_mod_rows(x_ref, g_ref, sc_ref, sh_ref, h_scr, inv_scr)
        o_ref[...] = jnp.zeros_like(o_ref)

    u = jnp.square(jnp.maximum(_dot(h_scr[...], wu_ref[...]), 0.0))
    o_ref[...] += _dot(u.astype(BF16), wd_ref[...])

    @pl.when(f == pl.num_programs(1) - 1)
    def _():
        tm = x_ref.shape[0]
        rows = min(tm, 256)

        def body(r, carry):
            sl = pl.ds(pl.multiple_of(r * rows, rows), rows)
            gt = gt_ref[...] if gt_ref.shape[0] == 1 else gt_ref[sl, :]
            xn = x_ref[sl, :] + gt * o_ref[sl, :]
            if final:
                xn = xn * lax.rsqrt(jnp.mean(xn * xn, axis=-1, keepdims=True) + EPS) * fg_ref[...]
            o_ref[sl, :] = xn
            return carry

        lax.fori_loop(0, tm // rows, body, 0)


def _ffn(x, g, mod, wu, wd, final_g, final, layer, rows_per_stream, tm=1024, tf=1024):
    n, d = x.shape
    dff = wu.shape[1]
    tm, tps = _row_tile(tm, n, mod, rows_per_stream)
    tf = _col_tile(tf, n, dff, 2048)
    return pl.pallas_call(
        functools.partial(_ffn_kernel, final=final),
        grid=(n // tm, dff // tf),
        in_specs=[pl.BlockSpec((tm, d), lambda i, j: (i, 0)),
                  pl.BlockSpec((None, 1, d), lambda i, j: (layer, 0, 0)),
                  _mod_spec(mod, 4, d, tm, tps),
                  _mod_spec(mod, 3, d, tm, tps),
                  _mod_spec(mod, 5, d, tm, tps),
                  pl.BlockSpec((d, tf), lambda i, j: (0, j)),
                  pl.BlockSpec((tf, d), lambda i, j: (j, 0)),
                  pl.BlockSpec((1, d), lambda i, j: (0, 0))],
        out_specs=pl.BlockSpec((tm, d), lambda i, j: (i, 0)),
        out_shape=jax.ShapeDtypeStruct((n, d), F32),
        scratch_shapes=[pltpu.VMEM((tm, d), BF16), pltpu.VMEM((tm, 1), F32)],
        compiler_params=_params("parallel", "arbitrary"),
        name="ffn_residual",
    )(x, g, mod, mod, mod, wu, wd, final_g)


def _trunk(x, mod_all, per_token_mod, conv_c, st_c, st_n, st_m, st_s, lb_all, wts, mats=None):
    cast_mats = []
    batch, seq, d = x.shape
    depth = conv_c.shape[0]
    L = min(MA_CHUNK, seq)
    Lh = min(HB_CHUNK, seq)
    nc = seq // L
    n = batch * seq
    nz = wts["b_in"].shape[2]
    keep = CONV_W - 1
    xf = x.reshape(n, d)
    bufs, cms, nvs, mrs, sms = [], [], [], [], []
    for l in range(depth):
        mod = mod_all[l]
        if per_token_mod:
            mod = jnp.repeat(mod, seq, axis=0)
        else:
            mod = mod[:, None, :]
        h, zg = _norm_gates(xf, wts["norm1_g"], mod, wts["w_gate"], wts["b_gate"], l, seq)
        if mats is None:
            z, layer_mats = _mm_in(h, wts["w_in"], wts["b_in"], l, nz, side=wts["mats_f32"])
            cast_mats.append(layer_mats)
        else:
            z, _ = _mm_in(h, wts["w_in"], wts["b_in"], l, nz)
            layer_mats = mats[l]
        w_br_a, w_br_b, w_o, w_up, w_down = layer_mats
        gates_t = jnp.swapaxes(zg[:, :2 * MA_HEADS].reshape(batch, nc, L, 2 * MA_HEADS), 2, 3)
        conv0 = jnp.pad(conv_c[l], ((0, 0), (CONV_PAD - keep, 0), (0, 0)))
        m0 = jnp.broadcast_to(st_m[l][:, :, None], (batch, MA_HEADS, GATE_PAD))
        ya, c1, n1, m1 = _mlstm(z, zg, gates_t, wts["conv_w"], wts["conv_b"], wts["ma_norm"],
                                conv0, st_c, st_n, m0, l, batch, seq, L)
        yb, s1 = _hgrn(z, lb_all, wts["hb_norm"], st_s, l, batch, seq, Lh)
        merged = _mm_branches(ya, yb, w_br_a, w_br_b, z,
                              4 * MA_WIDTH + 4 * HB_WIDTH, 4 * MA_WIDTH + 4 * HB_WIDTH + d, l)
        xf = _mm_residual(merged, w_o, xf, mod, 2, l, seq)
        xf = _ffn(xf, wts["norm2_g"], mod, w_up, w_down, wts["final_g"], l == depth - 1, l, seq)
        bufs.append(z.reshape(batch, seq, nz)[:, seq - keep:, :2 * MA_WIDTH])
        cms.append(c1)
        nvs.append(n1)
        mrs.append(m1[:, :, 0])
        sms.append(s1)
    outs = (xf.reshape(batch, seq, d), jnp.stack(bufs), jnp.stack(cms), jnp.stack(nvs),
            jnp.stack(mrs), jnp.stack(sms))
    return outs, (cast_mats if mats is None else mats)


def kernel(x_prompt, x_sample, cache_conv, state_mlstm_C, state_mlstm_n, state_mlstm_m, state_hgrn,
           c_prompt, c_sample, ada_w, ada_b, norm1_g, norm2_g, w_in, b_in, conv_w, conv_b, ma_norm,
           hgrn_lb_raw, hb_norm, w_br_a, w_br_b, w_o, w_up, w_down, final_g):
    depth, d, n_in = w_in.shape
    bp, bs = x_prompt.shape[0], x_sample.shape[0]
    n_main = n_in - 2 * MA_HEADS
    gate_pad = GATE_PAD - 2 * MA_HEADS

    lb_all = _lower_bounds(hgrn_lb_raw).reshape(depth, 1, HB_WIDTH)
    c_all = jnp.concatenate([c_prompt, c_sample], axis=0)
    c_rows = -(-c_all.shape[0] // 8) * 8
    c_all = jnp.pad(c_all, ((0, c_rows - c_all.shape[0]), (0, 0)))
    mod_all = _modulation(c_all, ada_w, ada_b)

    wts = {
        "norm1_g": norm1_g.reshape(depth, 1, d),
        "norm2_g": norm2_g.reshape(depth, 1, d),
        "w_in": w_in.astype(BF16),
        "b_in": b_in[:, :n_main].reshape(depth, 1, n_main),
        "w_gate": jnp.pad(w_in[:, :, n_main:], ((0, 0), (0, 0), (0, gate_pad))).astype(BF16),
        "b_gate": jnp.pad(b_in[:, n_main:], ((0, 0), (0, gate_pad))).reshape(depth, 1, GATE_PAD),
        "conv_w": conv_w,
        "conv_b": conv_b.reshape(depth, 1, -1),
        "ma_norm": ma_norm.reshape(depth, 1, -1),
        "hb_norm": hb_norm.reshape(depth, 1, -1),
        "mats_f32": (w_br_a, w_br_b, w_o, w_up, w_down),
        "final_g": final_g.reshape(1, d),
    }

    f32 = jnp.float32
    z_conv = jnp.zeros((depth, bp, CONV_W - 1, 2 * MA_WIDTH), f32)
    z_c = jnp.zeros((depth, bp, MA_HEADS, MA_HEAD_DIM, MA_HEAD_DIM), f32)
    z_n = jnp.zeros((depth, bp, MA_HEADS, MA_HEAD_DIM), f32)
    z_m = jnp.zeros((depth, bp, MA_HEADS), f32)
    z_s = jnp.zeros((depth, bp, HB_HEADS, HB_DIM, HB_DIM), f32)

    out_p, mats = _trunk(x_prompt, mod_all[:, :bp], False, z_conv, z_c, z_n, z_m, z_s, lb_all, wts)
    out_s, _ = _trunk(x_sample, mod_all[:, bp:bp + bs], True, cache_conv, state_mlstm_C, state_mlstm_n,
                      state_mlstm_m, state_hgrn, lb_all, wts, mats)
    y_p, conv_p, c_p, n_p, m_p, s_p = out_p
    y_s, conv_s, c_s, n_s, m_s, s_s = out_s
    return (y_p, y_s, conv_p, c_p, n_p, m_p, s_p, conv_s, c_s, n_s, m_s, s_s)
```

```python
import functools

import jax
import jax.numpy as jnp
from jax import lax
from jax.experimental import pallas as pl
from jax.experimental.pallas import tpu as pltpu

EPS = 1e-6
NEG = -1e30
MA_CHUNK = 256
HB_CHUNK = 64
HB_STEP_ROWS = 256
MA_HEADS = 8
MA_HEAD_DIM = 256
MA_WIDTH = MA_HEADS * MA_HEAD_DIM
CONV_W = 4
HB_HEADS = 16
HB_DIM = 128
HB_WIDTH = HB_HEADS * HB_DIM
GATE_PAD = 128
CONV_PAD = 8
BF16_ROWS = 16
STREAM_ROWS = 256
HGRN_SAFE_LOG2 = 115.0
VMEM_LIMIT = 62 * 1024 * 1024

F32 = jnp.float32
BF16 = jnp.bfloat16

_NT = (((1,), (1,)), ((), ()))
_TN = (((0,), (0,)), ((), ()))


def _params(*sem):
    return pltpu.CompilerParams(dimension_semantics=sem, vmem_limit_bytes=VMEM_LIMIT)


def _sigmoid(x):
    return 1.0 / (1.0 + jnp.exp(-x))


def _log_sigmoid(x):
    return jnp.minimum(x, 0.0) - jnp.log1p(jnp.exp(-jnp.abs(x)))


def _dot(a, b):
    return jnp.dot(a, b, preferred_element_type=F32)


def _lb_kernel(raw_ref, o_ref):
    x = raw_ref[...]
    e = jnp.exp(x - jnp.max(x, axis=0, keepdims=True))
    sm = e / jnp.sum(e, axis=0, keepdims=True)
    depth = x.shape[0]
    acc = sm[0:1, :]
    o_ref[0:1, :] = acc - sm[0:1, :]
    for l in range(1, depth):
        acc = acc + sm[l:l + 1, :]
        o_ref[l:l + 1, :] = acc - sm[0:1, :]


def _lower_bounds(raw):
    return pl.pallas_call(
        _lb_kernel, out_shape=jax.ShapeDtypeStruct(raw.shape, F32), name="hgrn_lower_bounds")(raw)


def _ada_kernel(c_ref, w_ref, b_ref, o_ref):
    c = c_ref[...]
    cs = (c * _sigmoid(c)).astype(BF16)
    o_ref[...] = _dot(cs, w_ref[...].astype(BF16)) + b_ref[...]


def _modulation(c_all, ada_w, ada_b, tn=1024):
    depth, d, n6 = ada_w.shape
    rows = c_all.shape[0]
    return pl.pallas_call(
        _ada_kernel,
        grid=(depth, n6 // tn),
        in_specs=[pl.BlockSpec((rows, d), lambda l, j: (0, 0)),
                  pl.BlockSpec((None, d, tn), lambda l, j: (l, 0, j)),
                  pl.BlockSpec((None, 1, tn), lambda l, j: (l, 0, j))],
        out_specs=pl.BlockSpec((None, rows, tn), lambda l, j: (l, 0, j)),
        out_shape=jax.ShapeDtypeStruct((depth, rows, n6), F32),
        compiler_params=_params("parallel", "parallel"),
        name="ada_modulation",
    )(c_all, ada_w, ada_b.reshape(depth, 1, n6))


def _norm_mod_rows(x_ref, g_ref, sc_ref, sh_ref, h_scr, inv_scr):
    tm = x_ref.shape[0]
    rows = min(tm, 256)

    def stats(r, carry):
        sl = pl.ds(pl.multiple_of(r * rows, rows), rows)
        x = x_ref[sl, :]
        inv_scr[sl, :] = lax.rsqrt(jnp.mean(x * x, axis=-1, keepdims=True) + EPS)
        return carry

    def scale(r, carry):
        sl = pl.ds(pl.multiple_of(r * rows, rows), rows)
        y = x_ref[sl, :] * inv_scr[sl, :] * g_ref[...]
        sc = sc_ref[...] if sc_ref.shape[0] == 1 else sc_ref[sl, :]
        sh = sh_ref[...] if sh_ref.shape[0] == 1 else sh_ref[sl, :]
        h_scr[sl, :] = (y * (1.0 + sc) + sh).astype(BF16)
        return carry

    lax.fori_loop(0, tm // rows, stats, 0)
    lax.fori_loop(0, tm // rows, scale, 0)


def _mod_spec(mod, k, d, tm, tiles_per_stream):
    if mod.ndim == 3:
        return pl.BlockSpec((None, 1, d), lambda i, j: (i // tiles_per_stream, 0, k))
    return pl.BlockSpec((tm, d), lambda i, j: (i, k))


def _col_tile(tn, rows, width, wide):
    t = min(wide if rows <= STREAM_ROWS else tn, width)
    while width % t:
        t //= 2
    return t


def _row_tile(tm, n, mod, rows_per_stream):
    tm = min(tm, n if mod.ndim == 2 else rows_per_stream)
    return tm, max(rows_per_stream // tm, 1)


def _norm_kernel(x_ref, g_ref, sc_ref, sh_ref, wg_ref, bg_ref, h_ref, zg_ref, inv_scr):
    _norm_mod_rows(x_ref, g_ref, sc_ref, sh_ref, h_ref, inv_scr)
    zg_ref[...] = _dot(h_ref[...], wg_ref[...]) + bg_ref[...]


def _norm_gates(x, g, mod, wg, bg, layer, rows_per_stream, tm=1024):
    n, d = x.shape
    tm, tps = _row_tile(tm, n, mod, rows_per_stream)
    return pl.pallas_call(
        _norm_kernel,
        grid=(n // tm, 1),
        in_specs=[pl.BlockSpec((tm, d), lambda i, j: (i, 0)),
                  pl.BlockSpec((None, 1, d), lambda i, j: (layer, 0, 0)),
                  _mod_spec(mod, 1, d, tm, tps),
                  _mod_spec(mod, 0, d, tm, tps),
                  pl.BlockSpec((None, d, GATE_PAD), lambda i, j: (layer, 0, 0)),
                  pl.BlockSpec((None, 1, GATE_PAD), lambda i, j: (layer, 0, 0))],
        out_specs=[pl.BlockSpec((tm, d), lambda i, j: (i, 0)),
                   pl.BlockSpec((tm, GATE_PAD), lambda i, j: (i, 0))],
        out_shape=[jax.ShapeDtypeStruct((n, d), BF16),
                   jax.ShapeDtypeStruct((n, GATE_PAD), F32)],
        scratch_shapes=[pltpu.VMEM((tm, 1), F32)],
        compiler_params=_params("parallel", "arbitrary"),
        name="norm_gates",
    )(x, g, mod, mod, wg, bg)


def _mm_in_kernel(h_ref, w_ref, b_ref, *rest):
    n_side = (len(rest) - 1) // 2
    z_ref = rest[n_side]
    z_ref[...] = lax.dot_general(h_ref[...], w_ref[...], _NT, preferred_element_type=F32) + b_ref[...]
    for f32_ref, bf16_ref in zip(rest[:n_side], rest[n_side + 1:]):
        bf16_ref[...] = f32_ref[...].astype(BF16)


def _mm_in(h, w, b, layer, nz, side=(), tm=2048, tn=1024):
    n, d = h.shape
    tm = min(tm, n)
    tn = _col_tile(tn, n, nz, 4096)
    nj = nz // tn
    steps = (n // tm) * nj
    side_in, side_out, side_shape = [], [], []
    for s, idx in side:
        rows, cols = s.shape[1:]
        slab = next(s for s in range(BF16_ROWS, rows + 1, BF16_ROWS) if rows % s == 0 and s * steps >= rows)
        last = rows // slab - 1
        side_in.append(pl.BlockSpec(
            (None, slab, cols), lambda i, j, last=last, idx=idx: (idx, jnp.minimum(i * nj + j, last), 0)))
        side_out.append(pl.BlockSpec(
            (slab, cols), lambda i, j, last=last: (jnp.minimum(i * nj + j, last), 0)))
        side_shape.append(jax.ShapeDtypeStruct((rows, cols), BF16))
    out = pl.pallas_call(
        _mm_in_kernel,
        grid=(n // tm, nj),
        in_specs=[pl.BlockSpec((tm, d), lambda i, j: (i, 0)),
                  pl.BlockSpec((None, tn, d), lambda i, j: (layer, j, 0)),
                  pl.BlockSpec((None, 1, tn), lambda i, j: (layer, 0, j))] + side_in,
        out_specs=[pl.BlockSpec((tm, tn), lambda i, j: (i, j))] + side_out,
        out_shape=[jax.ShapeDtypeStruct((n, nz), F32)] + side_shape,
        compiler_params=_params("arbitrary", "arbitrary"),
        name="in_proj",
    )(h, w, b, *[s for s, _ in side])
    return out[0], out[1:]


def _mlstm_kernel(zq_ref, zk_ref, zv_ref, zo_ref, zg_ref, gt_ref, cw_ref, cb_ref, nrm_ref,
                  conv0_ref, c0_ref, n0_ref, m0_ref,
                  y_ref, c_ref, n_ref, m_ref, hq_scr, hk_scr):
    L = zq_ref.shape[0]
    hd = MA_HEAD_DIM
    heads = range(MA_HEADS)
    sls = [slice(h * hd, (h + 1) * hd) for h in heads]

    @pl.when(pl.program_id(1) == 0)
    def _():
        c_ref[...] = c0_ref[...]
        n_ref[...] = n0_ref[...]
        m_ref[...] = m0_ref[...]
        hq_scr[...] = conv0_ref[0, :, 0:MA_WIDTH]
        hk_scr[...] = conv0_ref[0, :, MA_WIDTH:2 * MA_WIDTH]

    def conv_silu(x_ref, hist_scr, off):
        xs = jnp.concatenate([hist_scr[...], x_ref[...]], axis=0)
        acc = xs * cw_ref[0:1, off:off + MA_WIDTH]
        for j in range(1, CONV_W):
            acc = pltpu.roll(acc, 1, 0) + xs * cw_ref[j:j + 1, off:off + MA_WIDTH]
        acc = acc[CONV_PAD:, :] + cb_ref[:, off:off + MA_WIDTH]
        hist_scr[...] = x_ref[L - CONV_PAD:L, :]
        return acc * _sigmoid(acc)

    q = conv_silu(zq_ref, hq_scr, 0)
    k = conv_silu(zk_ref, hk_scr, MA_WIDTH) * (MA_HEAD_DIM ** -0.5)
    qb = q.astype(BF16)
    kb = k.astype(BF16)
    vb = zv_ref[...].astype(BF16)
    zg = zg_ref[...]
    gates_t = gt_ref[0, 0]

    ti = lax.broadcasted_iota(jnp.int32, (L, L), 0)
    si = lax.broadcasted_iota(jnp.int32, (L, L), 1)
    causal = si <= ti
    tril = causal.astype(BF16)

    def split3(x):
        hi = x.astype(BF16)
        r1 = x - hi.astype(F32)
        mid = r1.astype(BF16)
        return hi, mid, (r1 - mid.astype(F32)).astype(BF16)

    b_cols = sum(_dot(tril, part) for part in split3(_log_sigmoid(zg)))
    b_rows = sum(lax.dot_general(part, tril, _NT, preferred_element_type=F32)
                 for part in split3(_log_sigmoid(gates_t)))

    d, w_inter, w_last, decay, m_t, m_last = [], [], [], [], [], []
    for h in heads:
        ig_col = zg[:, h:h + 1]
        ig_row = gates_t[h:h + 1, :]
        b_col = b_cols[:, MA_HEADS + h:MA_HEADS + h + 1]
        b_row = b_rows[MA_HEADS + h:MA_HEADS + h + 1, :]
        log_d = jnp.where(causal, b_col - b_row + ig_row, NEG)
        log_inter = b_col + m_ref[0, h:h + 1, 0:1]
        m_h = jnp.maximum(log_inter, jnp.max(log_d, axis=1, keepdims=True))
        wi = jnp.exp(log_inter - m_h)
        ml = m_h[L - 1:L, :]
        d.append(jnp.exp(log_d - m_h))
        w_inter.append(wi)
        w_last.append(jnp.exp(b_col[L - 1:L, :] - b_col + ig_col - ml))
        decay.append(wi[L - 1:L, :])
        m_t.append(m_h)
        m_last.append(ml)

    s = [lax.dot_general(qb[:, sl], kb[:, sl], _NT, preferred_element_type=F32) * d[h]
         for h, sl in zip(heads, sls)]
    qc = [_dot(qb[:, sl], c_ref[0, h].astype(BF16)) for h, sl in zip(heads, sls)]
    sv = [_dot(s[h].astype(BF16), vb[:, sl]) for h, sl in zip(heads, sls)]
    kw = [k[:, sl] * w_last[h] for h, sl in zip(heads, sls)]
    upd = [lax.dot_general(kw[h].astype(BF16), vb[:, sl], _TN, preferred_element_type=F32)
           for h, sl in zip(heads, sls)]

    for h, sl in zip(heads, sls):
        n_row = n_ref[0, h:h + 1, :]
        qn = (jnp.sum(s[h], axis=1, keepdims=True)
              + w_inter[h] * jnp.sum(q[:, sl] * n_row, axis=1, keepdims=True))
        denom = jnp.maximum(jnp.abs(qn), jnp.exp(-m_t[h]))
        num = sv[h] + w_inter[h] * qc[h]
        y = num * lax.rsqrt(jnp.mean(num * num, axis=-1, keepdims=True) + EPS * denom * denom)
        y_ref[:, sl] = (y * nrm_ref[:, sl] * _sigmoid(zo_ref[:, sl])).astype(BF16)
        c_ref[0, h] = decay[h] * c_ref[0, h] + upd[h]
        n_ref[0, h:h + 1, :] = decay[h] * n_row + jnp.sum(kw[h], axis=0, keepdims=True)
        m_ref[0, h:h + 1, :] = jnp.broadcast_to(m_last[h], (1, m_ref.shape[2]))


def _mlstm(z, zg, gates_t, conv_w, conv_b, ma_norm, conv0, c0, n0, m0, layer, batch, seq, L):
    nc = seq // L
    w = MA_WIDTH
    blk = lambda col: pl.BlockSpec((L, w), lambda b, c: (b * nc + c, col))
    st4 = pl.BlockSpec((1, MA_HEADS, MA_HEAD_DIM, MA_HEAD_DIM), lambda b, c: (b, 0, 0, 0))
    st3 = pl.BlockSpec((1, MA_HEADS, MA_HEAD_DIM), lambda b, c: (b, 0, 0))
    stm = pl.BlockSpec((1, MA_HEADS, GATE_PAD), lambda b, c: (b, 0, 0))
    return pl.pallas_call(
        _mlstm_kernel,
        grid=(batch, nc),
        in_specs=[blk(0), blk(1), blk(2), blk(3),
                  pl.BlockSpec((L, GATE_PAD), lambda b, c: (b * nc + c, 0)),
                  pl.BlockSpec((1, 1, 2 * MA_HEADS, L), lambda b, c: (b, c, 0, 0)),
                  pl.BlockSpec((None, CONV_W, 2 * w), lambda b, c: (layer, 0, 0)),
                  pl.BlockSpec((None, 1, 2 * w), lambda b, c: (layer, 0, 0)),
                  pl.BlockSpec((None, 1, w), lambda b, c: (layer, 0, 0)),
                  pl.BlockSpec((1, CONV_PAD, 2 * w), lambda b, c: (b, 0, 0)),
                  pl.BlockSpec((None, 1, MA_HEADS, MA_HEAD_DIM, MA_HEAD_DIM),
                               lambda b, c: (layer, b, 0, 0, 0)),
                  pl.BlockSpec((None, 1, MA_HEADS, MA_HEAD_DIM), lambda b, c: (layer, b, 0, 0)),
                  stm],
        out_specs=[pl.BlockSpec((L, w), lambda b, c: (b * nc + c, 0)), st4, st3, stm],
        out_shape=[jax.ShapeDtypeStruct((batch * seq, w), BF16),
                   jax.ShapeDtypeStruct(c0.shape[1:], F32),
                   jax.ShapeDtypeStruct(n0.shape[1:], F32),
                   jax.ShapeDtypeStruct(m0.shape, F32)],
        scratch_shapes=[pltpu.VMEM((CONV_PAD, w), F32), pltpu.VMEM((CONV_PAD, w), F32)],
        compiler_params=_params("parallel", "arbitrary"),
        name="mlstm_scan",
    )(z, z, z, z, zg, gates_t, conv_w, conv_b, ma_norm, conv0, c0, n0, m0)


def _hgrn_kernel(zq_ref, zf_ref, zi_ref, zog_ref, lb_ref, nrm_ref, s0_ref,
                 y_ref, s_ref, st_scr, b_scr, k_scr, q_scr, o_scr):
    L = b_scr.shape[0]
    hd = HB_DIM
    heads = range(HB_HEADS)
    sls = [slice(h * hd, (h + 1) * hd) for h in heads]
    nlast = pl.num_programs(1) - 1

    @pl.when(pl.program_id(1) == 0)
    def _():
        for h in heads:
            st_scr[h] = s0_ref[0, h].T

    nsub = zq_ref.shape[0] // L
    blocks = [slice(u * L, (u + 1) * L) for u in range(nsub)]

    lb = lb_ref[...]
    fb = zf_ref[...]
    e_f = jnp.exp(-jnp.abs(fb))
    r_f = 1.0 / (1.0 + e_f)
    pos = fb >= 0.0
    log_f = jnp.log2(lb + (1.0 - lb) * jnp.where(pos, r_f, e_f * r_f))
    k = (1.0 - lb) * jnp.where(pos, e_f * r_f, r_f)
    qr = zq_ref[...]
    q = qr * _sigmoid(qr)
    vb = zi_ref[...].astype(BF16)

    ti = lax.broadcasted_iota(jnp.int32, (L, L), 0)
    si = lax.broadcasted_iota(jnp.int32, (L, L), 1)
    causal = si <= ti
    tril = causal.astype(BF16)
    f_hi = log_f.astype(BF16)
    r1 = log_f - f_hi.astype(F32)
    f_mid = r1.astype(BF16)
    f_lo = (r1 - f_mid.astype(F32)).astype(BF16)
    b = [_dot(tril, f_hi[rw, :]) + _dot(tril, f_mid[rw, :]) + _dot(tril, f_lo[rw, :]) for rw in blocks]

    mid = L // 2
    r = [bu[mid - 1:mid, :] for bu in b]
    b_last = [bu[L - 1:L, :] for bu in b]
    spread = [jnp.max(jnp.maximum(-ru, ru - bl)) for ru, bl in zip(r, b_last)]
    all_safe = functools.reduce(jnp.maximum, spread) <= HGRN_SAFE_LOG2

    e_last = [jnp.exp2(bl) for bl in b_last]
    qs = [(q[rw, :] * jnp.exp2(bu)).astype(BF16) for rw, bu in zip(blocks, b)]
    ks = [(k[rw, :] * jnp.exp2(bl - bu)).astype(BF16) for rw, bu, bl in zip(blocks, b, b_last)]
    upd = [[lax.dot_general(vb[rw, sl], ks[u][:, sl], _TN, preferred_element_type=F32) for sl in sls]
           for u, rw in enumerate(blocks)]

    def advance_state(u, intra):
        for h, sl in zip(heads, sls):
            o = lax.dot_general(qs[u][:, sl], st_scr[h].astype(BF16), _NT, preferred_element_type=F32)
            o_scr[blocks[u], sl] = o if intra is None else o + intra[h]
            st_scr[h] = e_last[u][:, sl] * st_scr[h] + upd[u][h]

    def intra_factorised(u):
        rw = blocks[u]
        qt = (q[rw, :] * jnp.exp2(b[u] - r[u])).astype(BF16)
        kt = (k[rw, :] * jnp.exp2(r[u] - b[u])).astype(BF16)
        a_all = [lax.dot_general(qt[:, sl], kt[:, sl], _NT, preferred_element_type=F32) for sl in sls]
        a_all = [jnp.where(causal, a, 0.0).astype(BF16) for a in a_all]
        return [_dot(a, vb[rw, sl]) for a, sl in zip(a_all, sls)]

    def intra_exact(u):
        rw = blocks[u]
        b_scr[...] = b[u]
        k_scr[...] = k[rw, :]
        q_scr[...] = q[rw, :]
        width = b_scr.shape[1]
        head_of_lane = lax.broadcasted_iota(jnp.int32, (width, GATE_PAD), 0) // hd
        col = lax.broadcasted_iota(jnp.int32, (width, GATE_PAD), 1)
        gather = (head_of_lane == col).astype(BF16)
        head_of_lane_t = lax.broadcasted_iota(jnp.int32, (GATE_PAD, width), 1) // hd
        row = lax.broadcasted_iota(jnp.int32, (GATE_PAD, width), 0)
        scatter = (head_of_lane_t == row).astype(BF16)
        t_idx = lax.broadcasted_iota(jnp.int32, (L, GATE_PAD), 0)

        def body(s, carry):
            b_s = b_scr[pl.ds(s, 1), :]
            k_s = k_scr[pl.ds(s, 1), :]
            v_s = zi_ref[pl.ds(rw.start + s, 1), :]
            e = jnp.exp2(jnp.minimum(b_scr[...] - b_s, 0.0)) * q_scr[...] * k_s
            a = jnp.where(t_idx >= s, _dot(e.astype(BF16), gather), 0.0)
            o_scr[rw, :] += _dot(a.astype(BF16), scatter) * v_s
            return carry

        lax.fori_loop(0, L, body, 0)

    @pl.when(all_safe)
    def _():
        intra = [intra_factorised(u) for u in range(nsub)]
        for u in range(nsub):
            advance_state(u, intra[u])

    @pl.when(jnp.logical_not(all_safe))
    def _():
        for u in range(nsub):
            advance_state(u, None)
            safe = spread[u] <= HGRN_SAFE_LOG2

            @pl.when(safe)
            def _():
                for sl, o in zip(sls, intra_factorised(u)):
                    o_scr[blocks[u], sl] += o

            @pl.when(jnp.logical_not(safe))
            def _():
                intra_exact(u)

    o_all = [o_scr[:, sl] for sl in sls]
    inv = [lax.rsqrt(jnp.mean(o * o, axis=-1, keepdims=True) + EPS) for o in o_all]
    for sl, o, iv in zip(sls, o_all, inv):
        y_ref[:, sl] = (o * iv * nrm_ref[:, sl] * _sigmoid(zog_ref[:, sl])).astype(BF16)

    @pl.when(pl.program_id(1) == nlast)
    def _():
        for h in heads:
            s_ref[0, h] = st_scr[h].T


def _hgrn(z, lb_all, hb_norm, s0, layer, batch, seq, L):
    rows = min(HB_STEP_ROWS, seq)
    nc = seq // rows
    w = HB_WIDTH
    blk = lambda col: pl.BlockSpec((rows, w), lambda b, c: (b * nc + c, col))
    st = pl.BlockSpec((1, HB_HEADS, HB_DIM, HB_DIM), lambda b, c: (b, 0, 0, 0))
    return pl.pallas_call(
        _hgrn_kernel,
        grid=(batch, nc),
        in_specs=[blk(4), blk(5), blk(6), blk(7),
                  pl.BlockSpec((None, 1, w), lambda b, c: (layer, 0, 0)),
                  pl.BlockSpec((None, 1, w), lambda b, c: (layer, 0, 0)),
                  pl.BlockSpec((None, 1, HB_HEADS, HB_DIM, HB_DIM), lambda b, c: (layer, b, 0, 0, 0))],
        out_specs=[pl.BlockSpec((rows, w), lambda b, c: (b * nc + c, 0)), st],
        out_shape=[jax.ShapeDtypeStruct((batch * seq, w), BF16),
                   jax.ShapeDtypeStruct(s0.shape[1:], F32)],
        scratch_shapes=[pltpu.VMEM((HB_HEADS, HB_DIM, HB_DIM), F32),
                        pltpu.VMEM((L, w), F32), pltpu.VMEM((L, w), F32),
                        pltpu.VMEM((L, w), F32), pltpu.VMEM((rows, w), F32)],
        compiler_params=_params("parallel", "arbitrary"),
        name="hgrn_scan",
    )(z, z, z, z, lb_all, hb_norm, s0)


def _mm_br_kernel(ya_ref, yb_ref, wa_ref, wb_ref, ga_ref, gb_ref, o_ref):
    ta = _dot(ya_ref[...], wa_ref[...])
    tb = _dot(yb_ref[...], wb_ref[...])
    o_ref[...] = (_sigmoid(ga_ref[...]) * ta + _sigmoid(gb_ref[...]) * tb).astype(BF16)


def _mm_branches(ya, yb, wa, wb, z, ga_col, gb_col, layer, tm=1024, tn=1024):
    n, kdim = ya.shape
    d = wa.shape[1]
    tm = min(tm, n)
    tn = _col_tile(tn, n, d, 2048)
    ga_blk, gb_blk = ga_col // tn, gb_col // tn
    return pl.pallas_call(
        _mm_br_kernel,
        grid=(n // tm, d // tn),
        in_specs=[pl.BlockSpec((tm, kdim), lambda i, j: (i, 0)),
                  pl.BlockSpec((tm, kdim), lambda i, j: (i, 0)),
                  pl.BlockSpec((kdim, tn), lambda i, j: (0, j)),
                  pl.BlockSpec((kdim, tn), lambda i, j: (0, j)),
                  pl.BlockSpec((tm, tn), lambda i, j: (i, ga_blk + j)),
                  pl.BlockSpec((tm, tn), lambda i, j: (i, gb_blk + j))],
        out_specs=pl.BlockSpec((tm, tn), lambda i, j: (i, j)),
        out_shape=jax.ShapeDtypeStruct((n, d), BF16),
        compiler_params=_params("parallel", "arbitrary"),
        name="branch_merge",
    )(ya, yb, wa, wb, z, z)


def _mm_res_kernel(a_ref, w_ref, x_ref, gt_ref, o_ref):
    o_ref[...] = x_ref[...] + gt_ref[...] * _dot(a_ref[...], w_ref[...])


def _mm_residual(a, w, x, mod, gate_chunk, layer, rows_per_stream, tm=2048, tn=512):
    n, kdim = a.shape
    d = w.shape[1]
    tm, tps = _row_tile(tm, n, mod, rows_per_stream)
    tn = _col_tile(tn, n, d, 2048)
    nj = d // tn
    if mod.ndim == 3:
        gate_spec = pl.BlockSpec((None, 1, tn), lambda i, j: (i // tps, 0, gate_chunk * nj + j))
    else:
        gate_spec = pl.BlockSpec((tm, tn), lambda i, j: (i, gate_chunk * nj + j))
    return pl.pallas_call(
        _mm_res_kernel,
        grid=(n // tm, nj),
        in_specs=[pl.BlockSpec((tm, kdim), lambda i, j: (i, 0)),
                  pl.BlockSpec((kdim, tn), lambda i, j: (0, j)),
                  pl.BlockSpec((tm, tn), lambda i, j: (i, j)),
                  gate_spec],
        out_specs=pl.BlockSpec((tm, tn), lambda i, j: (i, j)),
        out_shape=jax.ShapeDtypeStruct((n, d), F32),
        compiler_params=_params("parallel", "arbitrary"),
        name="out_proj_residual",
    )(a, w, x, mod)


def _ffn_kernel(x_ref, g_ref, sc_ref, sh_ref, gt_ref, wu_ref, wd_ref, fg_ref, o_ref, h_scr, inv_scr,
                *, final):
    f = pl.program_id(1)

    @pl.when(f == 0)
    def _():
        _norm_mod_rows(x_ref, g_ref, sc_ref, sh_ref, h_scr, inv_scr)
        o_ref[...] = jnp.zeros_like(o_ref)

    u = jnp.square(jnp.maximum(_dot(h_scr[...], wu_ref[...]), 0.0))
    o_ref[...] += _dot(u.astype(BF16), wd_ref[...])

    @pl.when(f == pl.num_programs(1) - 1)
    def _():
        tm = x_ref.shape[0]
        rows = min(tm, 256)

        def body(r, carry):
            sl = pl.ds(pl.multiple_of(r * rows, rows), rows)
            gt = gt_ref[...] if gt_ref.shape[0] == 1 else gt_ref[sl, :]
            xn = x_ref[sl, :] + gt * o_ref[sl, :]
            if final:
                xn = xn * lax.rsqrt(jnp.mean(xn * xn, axis=-1, keepdims=True) + EPS) * fg_ref[...]
            o_ref[sl, :] = xn
            return carry

        lax.fori_loop(0, tm // rows, body, 0)


def _ffn(x, g, mod, wu, wd, final_g, final, layer, rows_per_stream, tm=1024, tf=1024):
    n, d = x.shape
    dff = wu.shape[1]
    tm, tps = _row_tile(tm, n, mod, rows_per_stream)
    tf = _col_tile(tf, n, dff, 2048)
    return pl.pallas_call(
        functools.partial(_ffn_kernel, final=final),
        grid=(n // tm, dff // tf),
        in_specs=[pl.BlockSpec((tm, d), lambda i, j: (i, 0)),
                  pl.BlockSpec((None, 1, d), lambda i, j: (layer, 0, 0)),
                  _mod_spec(mod, 4, d, tm, tps),
                  _mod_spec(mod, 3, d, tm, tps),
                  _mod_spec(mod, 5, d, tm, tps),
                  pl.BlockSpec((d, tf), lambda i, j: (0, j)),
                  pl.BlockSpec((tf, d), lambda i, j: (j, 0)),
                  pl.BlockSpec((1, d), lambda i, j: (0, 0))],
        out_specs=pl.BlockSpec((tm, d), lambda i, j: (i, 0)),
        out_shape=jax.ShapeDtypeStruct((n, d), F32),
        scratch_shapes=[pltpu.VMEM((tm, d), BF16), pltpu.VMEM((tm, 1), F32)],
        compiler_params=_params("parallel", "arbitrary"),
        name="ffn_residual",
    )(x, g, mod, mod, mod, wu, wd, final_g)


def _trunk(x, mod_all, per_token_mod, conv_c, st_c, st_n, st_m, st_s, lb_all, wts, mats=None):
    cast_mats = []
    batch, seq, d = x.shape
    depth = conv_c.shape[0]
    L = min(MA_CHUNK, seq)
    Lh = min(HB_CHUNK, seq)
    nc = seq // L
    n = batch * seq
    nz = wts["b_in"].shape[2]
    keep = CONV_W - 1
    xf = x.reshape(n, d)
    bufs, cms, nvs, mrs, sms = [], [], [], [], []
    for l in range(depth):
        mod = mod_all[l]
        if per_token_mod:
            mod = jnp.repeat(mod, seq, axis=0)
        else:
            mod = mod[:, None, :]
        h, zg = _norm_gates(xf, wts["norm1_g"], mod, wts["w_gate"], wts["b_gate"], l, seq)
        if mats is None:
            z, layer_mats = _mm_in(h, wts["w_in_t"], wts["b_in"], l, nz, side=[(w, l) for w in wts["mats_f32"]])
            cast_mats.append(layer_mats)
        else:
            layer_mats = mats[l]
            z, _ = _mm_in(h, wts["w_in_t"], wts["b_in"], l, nz)
        w_br_a, w_br_b, w_o, w_up, w_down = layer_mats
        gates_t = jnp.swapaxes(zg[:, :2 * MA_HEADS].reshape(batch, nc, L, 2 * MA_HEADS), 2, 3)
        conv0 = jnp.pad(conv_c[l], ((0, 0), (CONV_PAD - keep, 0), (0, 0)))
        m0 = jnp.broadcast_to(st_m[l][:, :, None], (batch, MA_HEADS, GATE_PAD))
        ya, c1, n1, m1 = _mlstm(z, zg, gates_t, wts["conv_w"], wts["conv_b"], wts["ma_norm"],
                                conv0, st_c, st_n, m0, l, batch, seq, L)
        yb, s1 = _hgrn(z, lb_all, wts["hb_norm"], st_s, l, batch, seq, Lh)
        merged = _mm_branches(ya, yb, w_br_a, w_br_b, z,
                              4 * MA_WIDTH + 4 * HB_WIDTH, 4 * MA_WIDTH + 4 * HB_WIDTH + d, l)
        xf = _mm_residual(merged, w_o, xf, mod, 2, l, seq)
        xf = _ffn(xf, wts["norm2_g"], mod, w_up, w_down, wts["final_g"], l == depth - 1, l, seq)
        bufs.append(z.reshape(batch, seq, nz)[:, seq - keep:, :2 * MA_WIDTH])
        cms.append(c1)
        nvs.append(n1)
        mrs.append(m1[:, :, 0])
        sms.append(s1)
    outs = (xf.reshape(batch, seq, d), jnp.stack(bufs), jnp.stack(cms), jnp.stack(nvs),
            jnp.stack(mrs), jnp.stack(sms))
    return outs, (cast_mats if mats is None else mats)


def kernel(x_prompt, x_sample, cache_conv, state_mlstm_C, state_mlstm_n, state_mlstm_m, state_hgrn,
           c_prompt, c_sample, ada_w, ada_b, norm1_g, norm2_g, w_in, b_in, conv_w, conv_b, ma_norm,
           hgrn_lb_raw, hb_norm, w_br_a, w_br_b, w_o, w_up, w_down, final_g):
    depth, d, n_in = w_in.shape
    bp, bs = x_prompt.shape[0], x_sample.shape[0]
    n_main = n_in - 2 * MA_HEADS
    gate_pad = GATE_PAD - 2 * MA_HEADS

    lb_all = _lower_bounds(hgrn_lb_raw).reshape(depth, 1, HB_WIDTH)
    c_all = jnp.concatenate([c_prompt, c_sample], axis=0)
    c_rows = -(-c_all.shape[0] // 8) * 8
    c_all = jnp.pad(c_all, ((0, c_rows - c_all.shape[0]), (0, 0)))
    mod_all = _modulation(c_all, ada_w, ada_b)

    w_in_t = jnp.swapaxes(w_in, 1, 2)
    wts = {
        "norm1_g": norm1_g.reshape(depth, 1, d),
        "norm2_g": norm2_g.reshape(depth, 1, d),
        "w_in_t": w_in_t.astype(BF16),
        "b_in": b_in[:, :n_main].reshape(depth, 1, n_main),
        "w_gate": jnp.pad(jnp.swapaxes(w_in_t[:, n_main:, :], 1, 2),
                          ((0, 0), (0, 0), (0, gate_pad))).astype(BF16),
        "b_gate": jnp.pad(b_in[:, n_main:], ((0, 0), (0, gate_pad))).reshape(depth, 1, GATE_PAD),
        "conv_w": conv_w,
        "conv_b": conv_b.reshape(depth, 1, -1),
        "ma_norm": ma_norm.reshape(depth, 1, -1),
        "hb_norm": hb_norm.reshape(depth, 1, -1),
        "mats_f32": (w_br_a, w_br_b, w_o, w_up, w_down),
        "final_g": final_g.reshape(1, d),
    }

    f32 = jnp.float32
    z_conv = jnp.zeros((depth, bp, CONV_W - 1, 2 * MA_WIDTH), f32)
    z_c = jnp.zeros((depth, bp, MA_HEADS, MA_HEAD_DIM, MA_HEAD_DIM), f32)
    z_n = jnp.zeros((depth, bp, MA_HEADS, MA_HEAD_DIM), f32)
    z_m = jnp.zeros((depth, bp, MA_HEADS), f32)
    z_s = jnp.zeros((depth, bp, HB_HEADS, HB_DIM, HB_DIM), f32)

    out_p, mats = _trunk(x_prompt, mod_all[:, :bp], False, z_conv, z_c, z_n, z_m, z_s, lb_all, wts)
    out_s, _ = _trunk(x_sample, mod_all[:, bp:bp + bs], True, cache_conv, state_mlstm_C, state_mlstm_n,
                      state_mlstm_m, state_hgrn, lb_all, wts, mats)
    y_p, conv_p, c_p, n_p, m_p, s_p = out_p
    y_s, conv_s, c_s, n_s, m_s, s_s = out_s
    return (y_p, y_s, conv_p, c_p, n_p, m_p, s_p, conv_s, c_s, n_s, m_s, s_s)
```

```python
import functools

import jax
import jax.numpy as jnp
from jax import lax
from jax.experimental import pallas as pl
from jax.experimental.pallas import tpu as pltpu

EPS = 1e-6
NEG = -1e30
MA_CHUNK = 256
HB_CHUNK = 64
HB_STEP_ROWS = 256
MA_HEADS = 8
MA_HEAD_DIM = 256
MA_WIDTH = MA_HEADS * MA_HEAD_DIM
CONV_W = 4
HB_HEADS = 16
HB_DIM = 128
HB_WIDTH = HB_HEADS * HB_DIM
GATE_PAD = 128
CONV_PAD = 8
BF16_ROWS = 16
STREAM_ROWS = 256
HGRN_SAFE_LOG2 = 115.0
VMEM_LIMIT = 62 * 1024 * 1024

F32 = jnp.float32
BF16 = jnp.bfloat16

_NT = (((1,), (1,)), ((), ()))
_TN = (((0,), (0,)), ((), ()))


def _params(*sem):
    return pltpu.CompilerParams(dimension_semantics=sem, vmem_limit_bytes=VMEM_LIMIT)


def _sigmoid(x):
    return 1.0 / (1.0 + jnp.exp(-x))


def _log_sigmoid(x):
    return jnp.minimum(x, 0.0) - jnp.log1p(jnp.exp(-jnp.abs(x)))


def _dot(a, b):
    return jnp.dot(a, b, preferred_element_type=F32)


def _lb_kernel(raw_ref, o_ref):
    x = raw_ref[...]
    e = jnp.exp(x - jnp.max(x, axis=0, keepdims=True))
    sm = e / jnp.sum(e, axis=0, keepdims=True)
    depth = x.shape[0]
    acc = sm[0:1, :]
    o_ref[0:1, :] = acc - sm[0:1, :]
    for l in range(1, depth):
        acc = acc + sm[l:l + 1, :]
        o_ref[l:l + 1, :] = acc - sm[0:1, :]


def _lower_bounds(raw):
    return pl.pallas_call(
        _lb_kernel, out_shape=jax.ShapeDtypeStruct(raw.shape, F32), name="hgrn_lower_bounds")(raw)


def _ada_kernel(c_ref, w_ref, b_ref, o_ref):
    c = c_ref[...]
    cs = (c * _sigmoid(c)).astype(BF16)
    o_ref[...] = _dot(cs, w_ref[...].astype(BF16)) + b_ref[...]


def _modulation(c_all, ada_w, ada_b, tn=1024):
    depth, d, n6 = ada_w.shape
    rows = c_all.shape[0]
    return pl.pallas_call(
        _ada_kernel,
        grid=(depth, n6 // tn),
        in_specs=[pl.BlockSpec((rows, d), lambda l, j: (0, 0)),
                  pl.BlockSpec((None, d, tn), lambda l, j: (l, 0, j)),
                  pl.BlockSpec((None, 1, tn), lambda l, j: (l, 0, j))],
        out_specs=pl.BlockSpec((None, rows, tn), lambda l, j: (l, 0, j)),
        out_shape=jax.ShapeDtypeStruct((depth, rows, n6), F32),
        compiler_params=_params("parallel", "parallel"),
        name="ada_modulation",
    )(c_all, ada_w, ada_b.reshape(depth, 1, n6))


def _norm_mod_rows(x_ref, g_ref, sc_ref, sh_ref, h_scr, inv_scr):
    tm = x_ref.shape[0]
    rows = min(tm, 256)

    def stats(r, carry):
        sl = pl.ds(pl.multiple_of(r * rows, rows), rows)
        x = x_ref[sl, :]
        inv_scr[sl, :] = lax.rsqrt(jnp.mean(x * x, axis=-1, keepdims=True) + EPS)
        return carry

    def scale(r, carry):
        sl = pl.ds(pl.multiple_of(r * rows, rows), rows)
        y = x_ref[sl, :] * inv_scr[sl, :] * g_ref[...]
        sc = sc_ref[...] if sc_ref.shape[0] == 1 else sc_ref[sl, :]
        sh = sh_ref[...] if sh_ref.shape[0] == 1 else sh_ref[sl, :]
        h_scr[sl, :] = (y * (1.0 + sc) + sh).astype(BF16)
        return carry

    lax.fori_loop(0, tm // rows, stats, 0)
    lax.fori_loop(0, tm // rows, scale, 0)


def _mod_spec(mod, k, d, tm, tiles_per_stream):
    if mod.ndim == 3:
        return pl.BlockSpec((None, 1, d), lambda i, j: (i // tiles_per_stream, 0, k))
    return pl.BlockSpec((tm, d), lambda i, j: (i, k))


def _col_tile(tn, rows, width, wide):
    t = min(wide if rows <= STREAM_ROWS else tn, width)
    while width % t:
        t //= 2
    return t


def _row_tile(tm, n, mod, rows_per_stream):
    tm = min(tm, n if mod.ndim == 2 else rows_per_stream)
    return tm, max(rows_per_stream // tm, 1)


def _norm_kernel(x_ref, g_ref, sc_ref, sh_ref, wg_ref, bg_ref, h_ref, zg_ref, inv_scr):
    _norm_mod_rows(x_ref, g_ref, sc_ref, sh_ref, h_ref, inv_scr)
    zg_ref[...] = _dot(h_ref[...], wg_ref[...]) + bg_ref[...]


def _norm_gates(x, g, mod, wg, bg, layer, rows_per_stream, tm=1024):
    n, d = x.shape
    tm, tps = _row_tile(tm, n, mod, rows_per_stream)
    return pl.pallas_call(
        _norm_kernel,
        grid=(n // tm, 1),
        in_specs=[pl.BlockSpec((tm, d), lambda i, j: (i, 0)),
                  pl.BlockSpec((None, 1, d), lambda i, j: (layer, 0, 0)),
                  _mod_spec(mod, 1, d, tm, tps),
                  _mod_spec(mod, 0, d, tm, tps),
                  pl.BlockSpec((None, d, GATE_PAD), lambda i, j: (layer, 0, 0)),
                  pl.BlockSpec((None, 1, GATE_PAD), lambda i, j: (layer, 0, 0))],
        out_specs=[pl.BlockSpec((tm, d), lambda i, j: (i, 0)),
                   pl.BlockSpec((tm, GATE_PAD), lambda i, j: (i, 0))],
        out_shape=[jax.ShapeDtypeStruct((n, d), BF16),
                   jax.ShapeDtypeStruct((n, GATE_PAD), F32)],
        scratch_shapes=[pltpu.VMEM((tm, 1), F32)],
        compiler_params=_params("parallel", "arbitrary"),
        name="norm_gates",
    )(x, g, mod, mod, wg, bg)


def _mm_in_kernel(h_ref, w_ref, b_ref, *rest):
    n_side = (len(rest) - 1) // 2
    z_ref = rest[n_side]
    z_ref[...] = lax.dot_general(h_ref[...], w_ref[...], _NT, preferred_element_type=F32) + b_ref[...]
    for f32_ref, bf16_ref in zip(rest[:n_side], rest[n_side + 1:]):
        bf16_ref[...] = f32_ref[...].astype(BF16)


def _mm_in(h, w, b, layer, nz, side=(), tm=1024, tn=2048):
    n, d = h.shape
    tm = min(tm, n)
    tn = _col_tile(tn, n, nz, 4096)
    nj = nz // tn
    steps = (n // tm) * nj
    side_in, side_out, side_shape = [], [], []
    for s, idx in side:
        rows, cols = s.shape[1:]
        slab = next(s for s in range(BF16_ROWS, rows + 1, BF16_ROWS) if rows % s == 0 and s * steps >= rows)
        last = rows // slab - 1
        side_in.append(pl.BlockSpec(
            (None, slab, cols), lambda i, j, last=last, idx=idx: (idx, jnp.minimum(i * nj + j, last), 0)))
        side_out.append(pl.BlockSpec(
            (slab, cols), lambda i, j, last=last: (jnp.minimum(i * nj + j, last), 0)))
        side_shape.append(jax.ShapeDtypeStruct((rows, cols), BF16))
    out = pl.pallas_call(
        _mm_in_kernel,
        grid=(n // tm, nj),
        in_specs=[pl.BlockSpec((tm, d), lambda i, j: (i, 0)),
                  pl.BlockSpec((None, tn, d), lambda i, j: (layer, j, 0)),
                  pl.BlockSpec((None, 1, tn), lambda i, j: (layer, 0, j))] + side_in,
        out_specs=[pl.BlockSpec((tm, tn), lambda i, j: (i, j))] + side_out,
        out_shape=[jax.ShapeDtypeStruct((n, nz), F32)] + side_shape,
        compiler_params=_params("arbitrary", "arbitrary"),
        name="in_proj",
    )(h, w, b, *[s for s, _ in side])
    return out[0], out[1:]


def _mlstm_kernel(zq_ref, zk_ref, zv_ref, zo_ref, zg_ref, gt_ref, cw_ref, cb_ref, nrm_ref,
                  conv0_ref, c0_ref, n0_ref, m0_ref,
                  y_ref, c_ref, n_ref, m_ref, hq_scr, hk_scr):
    L = zq_ref.shape[0]
    hd = MA_HEAD_DIM
    heads = range(MA_HEADS)
    sls = [slice(h * hd, (h + 1) * hd) for h in heads]

    @pl.when(pl.program_id(1) == 0)
    def _():
        c_ref[...] = c0_ref[...]
        n_ref[...] = n0_ref[...]
        m_ref[...] = m0_ref[...]
        hq_scr[...] = conv0_ref[0, :, 0:MA_WIDTH]
        hk_scr[...] = conv0_ref[0, :, MA_WIDTH:2 * MA_WIDTH]

    def conv_silu(x_ref, hist_scr, off):
        xs = jnp.concatenate([hist_scr[...], x_ref[...]], axis=0)
        acc = xs * cw_ref[0:1, off:off + MA_WIDTH]
        for j in range(1, CONV_W):
            acc = pltpu.roll(acc, 1, 0) + xs * cw_ref[j:j + 1, off:off + MA_WIDTH]
        acc = acc[CONV_PAD:, :] + cb_ref[:, off:off + MA_WIDTH]
        hist_scr[...] = x_ref[L - CONV_PAD:L, :]
        return acc * _sigmoid(acc)

    q = conv_silu(zq_ref, hq_scr, 0)
    k = conv_silu(zk_ref, hk_scr, MA_WIDTH) * (MA_HEAD_DIM ** -0.5)
    qb = q.astype(BF16)
    kb = k.astype(BF16)
    vb = zv_ref[...].astype(BF16)
    zg = zg_ref[...]
    gates_t = gt_ref[0, 0]

    ti = lax.broadcasted_iota(jnp.int32, (L, L), 0)
    si = lax.broadcasted_iota(jnp.int32, (L, L), 1)
    causal = si <= ti
    tril = causal.astype(BF16)

    def split3(x):
        hi = x.astype(BF16)
        r1 = x - hi.astype(F32)
        mid = r1.astype(BF16)
        return hi, mid, (r1 - mid.astype(F32)).astype(BF16)

    b_cols = sum(_dot(tril, part) for part in split3(_log_sigmoid(zg)))
    b_rows = sum(lax.dot_general(part, tril, _NT, preferred_element_type=F32)
                 for part in split3(_log_sigmoid(gates_t)))

    d, w_inter, w_last, decay, m_t, m_last = [], [], [], [], [], []
    for h in heads:
        ig_col = zg[:, h:h + 1]
        ig_row = gates_t[h:h + 1, :]
        b_col = b_cols[:, MA_HEADS + h:MA_HEADS + h + 1]
        b_row = b_rows[MA_HEADS + h:MA_HEADS + h + 1, :]
        log_d = jnp.where(causal, b_col - b_row + ig_row, NEG)
        log_inter = b_col + m_ref[0, h:h + 1, 0:1]
        m_h = jnp.maximum(log_inter, jnp.max(log_d, axis=1, keepdims=True))
        wi = jnp.exp(log_inter - m_h)
        ml = m_h[L - 1:L, :]
        d.append(jnp.exp(log_d - m_h))
        w_inter.append(wi)
        w_last.append(jnp.exp(b_col[L - 1:L, :] - b_col + ig_col - ml))
        decay.append(wi[L - 1:L, :])
        m_t.append(m_h)
        m_last.append(ml)

    s = [lax.dot_general(qb[:, sl], kb[:, sl], _NT, preferred_element_type=F32) * d[h]
         for h, sl in zip(heads, sls)]
    qc = [_dot(qb[:, sl], c_ref[0, h].astype(BF16)) for h, sl in zip(heads, sls)]
    sv = [_dot(s[h].astype(BF16), vb[:, sl]) for h, sl in zip(heads, sls)]
    kw = [k[:, sl] * w_last[h] for h, sl in zip(heads, sls)]
    upd = [lax.dot_general(kw[h].astype(BF16), vb[:, sl], _TN, preferred_element_type=F32)
           for h, sl in zip(heads, sls)]

    for h, sl in zip(heads, sls):
        n_row = n_ref[0, h:h + 1, :]
        qn = (jnp.sum(s[h], axis=1, keepdims=True)
              + w_inter[h] * jnp.sum(q[:, sl] * n_row, axis=1, keepdims=True))
        denom = jnp.maximum(jnp.abs(qn), jnp.exp(-m_t[h]))
        num = sv[h] + w_inter[h] * qc[h]
        y = num * lax.rsqrt(jnp.mean(num * num, axis=-1, keepdims=True) + EPS * denom * denom)
        y_ref[:, sl] = (y * nrm_ref[:, sl] * _sigmoid(zo_ref[:, sl])).astype(BF16)
        c_ref[0, h] = decay[h] * c_ref[0, h] + upd[h]
        n_ref[0, h:h + 1, :] = decay[h] * n_row + jnp.sum(kw[h], axis=0, keepdims=True)
        m_ref[0, h:h + 1, :] = jnp.broadcast_to(m_last[h], (1, m_ref.shape[2]))


def _mlstm(z, zg, gates_t, conv_w, conv_b, ma_norm, conv0, c0, n0, m0, layer, batch, seq, L):
    nc = seq // L
    w = MA_WIDTH
    blk = lambda col: pl.BlockSpec((L, w), lambda b, c: (b * nc + c, col))
    st4 = pl.BlockSpec((1, MA_HEADS, MA_HEAD_DIM, MA_HEAD_DIM), lambda b, c: (b, 0, 0, 0))
    st3 = pl.BlockSpec((1, MA_HEADS, MA_HEAD_DIM), lambda b, c: (b, 0, 0))
    stm = pl.BlockSpec((1, MA_HEADS, GATE_PAD), lambda b, c: (b, 0, 0))
    return pl.pallas_call(
        _mlstm_kernel,
        grid=(batch, nc),
        in_specs=[blk(0), blk(1), blk(2), blk(3),
                  pl.BlockSpec((L, GATE_PAD), lambda b, c: (b * nc + c, 0)),
                  pl.BlockSpec((1, 1, 2 * MA_HEADS, L), lambda b, c: (b, c, 0, 0)),
                  pl.BlockSpec((None, CONV_W, 2 * w), lambda b, c: (layer, 0, 0)),
                  pl.BlockSpec((None, 1, 2 * w), lambda b, c: (layer, 0, 0)),
                  pl.BlockSpec((None, 1, w), lambda b, c: (layer, 0, 0)),
                  pl.BlockSpec((1, CONV_PAD, 2 * w), lambda b, c: (b, 0, 0)),
                  pl.BlockSpec((None, 1, MA_HEADS, MA_HEAD_DIM, MA_HEAD_DIM),
                               lambda b, c: (layer, b, 0, 0, 0)),
                  pl.BlockSpec((None, 1, MA_HEADS, MA_HEAD_DIM), lambda b, c: (layer, b, 0, 0)),
                  stm],
        out_specs=[pl.BlockSpec((L, w), lambda b, c: (b * nc + c, 0)), st4, st3, stm],
        out_shape=[jax.ShapeDtypeStruct((batch * seq, w), BF16),
                   jax.ShapeDtypeStruct(c0.shape[1:], F32),
                   jax.ShapeDtypeStruct(n0.shape[1:], F32),
                   jax.ShapeDtypeStruct(m0.shape, F32)],
        scratch_shapes=[pltpu.VMEM((CONV_PAD, w), F32), pltpu.VMEM((CONV_PAD, w), F32)],
        compiler_params=_params("parallel", "arbitrary"),
        name="mlstm_scan",
    )(z, z, z, z, zg, gates_t, conv_w, conv_b, ma_norm, conv0, c0, n0, m0)


def _hgrn_kernel(zq_ref, zf_ref, zi_ref, zog_ref, lb_ref, nrm_ref, s0_ref,
                 y_ref, s_ref, st_scr, b_scr, k_scr, q_scr, o_scr):
    L = b_scr.shape[0]
    hd = HB_DIM
    heads = range(HB_HEADS)
    sls = [slice(h * hd, (h + 1) * hd) for h in heads]
    nlast = pl.num_programs(1) - 1

    @pl.when(pl.program_id(1) == 0)
    def _():
        for h in heads:
            st_scr[h] = s0_ref[0, h].T

    nsub = zq_ref.shape[0] // L
    blocks = [slice(u * L, (u + 1) * L) for u in range(nsub)]

    lb = lb_ref[...]
    fb = zf_ref[...]
    e_f = jnp.exp(-jnp.abs(fb))
    r_f = 1.0 / (1.0 + e_f)
    pos = fb >= 0.0
    log_f = jnp.log2(lb + (1.0 - lb) * jnp.where(pos, r_f, e_f * r_f))
    k = (1.0 - lb) * jnp.where(pos, e_f * r_f, r_f)
    qr = zq_ref[...]
    q = qr * _sigmoid(qr)
    vb = zi_ref[...].astype(BF16)

    ti = lax.broadcasted_iota(jnp.int32, (L, L), 0)
    si = lax.broadcasted_iota(jnp.int32, (L, L), 1)
    causal = si <= ti
    tril = causal.astype(BF16)
    f_hi = log_f.astype(BF16)
    r1 = log_f - f_hi.astype(F32)
    f_mid = r1.astype(BF16)
    f_lo = (r1 - f_mid.astype(F32)).astype(BF16)
    b = [_dot(tril, f_hi[rw, :]) + _dot(tril, f_mid[rw, :]) + _dot(tril, f_lo[rw, :]) for rw in blocks]

    mid = L // 2
    r = [bu[mid - 1:mid, :] for bu in b]
    b_last = [bu[L - 1:L, :] for bu in b]
    spread = [jnp.max(jnp.maximum(-ru, ru - bl)) for ru, bl in zip(r, b_last)]
    all_safe = functools.reduce(jnp.maximum, spread) <= HGRN_SAFE_LOG2

    e_last = [jnp.exp2(bl) for bl in b_last]
    qs = [(q[rw, :] * jnp.exp2(bu)).astype(BF16) for rw, bu in zip(blocks, b)]
    ks = [(k[rw, :] * jnp.exp2(bl - bu)).astype(BF16) for rw, bu, bl in zip(blocks, b, b_last)]
    upd = [[lax.dot_general(vb[rw, sl], ks[u][:, sl], _TN, preferred_element_type=F32) for sl in sls]
           for u, rw in enumerate(blocks)]

    def advance_state(u, intra):
        for h, sl in zip(heads, sls):
            o = lax.dot_general(qs[u][:, sl], st_scr[h].astype(BF16), _NT, preferred_element_type=F32)
            o_scr[blocks[u], sl] = o if intra is None else o + intra[h]
            st_scr[h] = e_last[u][:, sl] * st_scr[h] + upd[u][h]

    def intra_factorised(u):
        rw = blocks[u]
        qt = (q[rw, :] * jnp.exp2(b[u] - r[u])).astype(BF16)
        kt = (k[rw, :] * jnp.exp2(r[u] - b[u])).astype(BF16)
        a_all = [lax.dot_general(qt[:, sl], kt[:, sl], _NT, preferred_element_type=F32) for sl in sls]
        a_all = [jnp.where(causal, a, 0.0).astype(BF16) for a in a_all]
        return [_dot(a, vb[rw, sl]) for a, sl in zip(a_all, sls)]

    def intra_exact(u):
        rw = blocks[u]
        b_scr[...] = b[u]
        k_scr[...] = k[rw, :]
        q_scr[...] = q[rw, :]
        width = b_scr.shape[1]
        head_of_lane = lax.broadcasted_iota(jnp.int32, (width, GATE_PAD), 0) // hd
        col = lax.broadcasted_iota(jnp.int32, (width, GATE_PAD), 1)
        gather = (head_of_lane == col).astype(BF16)
        head_of_lane_t = lax.broadcasted_iota(jnp.int32, (GATE_PAD, width), 1) // hd
        row = lax.broadcasted_iota(jnp.int32, (GATE_PAD, width), 0)
        scatter = (head_of_lane_t == row).astype(BF16)
        t_idx = lax.broadcasted_iota(jnp.int32, (L, GATE_PAD), 0)

        def body(s, carry):
            b_s = b_scr[pl.ds(s, 1), :]
            k_s = k_scr[pl.ds(s, 1), :]
            v_s = zi_ref[pl.ds(rw.start + s, 1), :]
            e = jnp.exp2(jnp.minimum(b_scr[...] - b_s, 0.0)) * q_scr[...] * k_s
            a = jnp.where(t_idx >= s, _dot(e.astype(BF16), gather), 0.0)
            o_scr[rw, :] += _dot(a.astype(BF16), scatter) * v_s
            return carry

        lax.fori_loop(0, L, body, 0)

    @pl.when(all_safe)
    def _():
        intra = [intra_factorised(u) for u in range(nsub)]
        for u in range(nsub):
            advance_state(u, intra[u])

    @pl.when(jnp.logical_not(all_safe))
    def _():
        for u in range(nsub):
            advance_state(u, None)
            safe = spread[u] <= HGRN_SAFE_LOG2

            @pl.when(safe)
            def _():
                for sl, o in zip(sls, intra_factorised(u)):
                    o_scr[blocks[u], sl] += o

            @pl.when(jnp.logical_not(safe))
            def _():
                intra_exact(u)

    o_all = [o_scr[:, sl] for sl in sls]
    inv = [lax.rsqrt(jnp.mean(o * o, axis=-1, keepdims=True) + EPS) for o in o_all]
    for sl, o, iv in zip(sls, o_all, inv):
        y_ref[:, sl] = (o * iv * nrm_ref[:, sl] * _sigmoid(zog_ref[:, sl])).astype(BF16)

    @pl.when(pl.program_id(1) == nlast)
    def _():
        for h in heads:
            s_ref[0, h] = st_scr[h].T


def _hgrn(z, lb_all, hb_norm, s0, layer, batch, seq, L):
    rows = min(HB_STEP_ROWS, seq)
    nc = seq // rows
    w = HB_WIDTH
    blk = lambda col: pl.BlockSpec((rows, w), lambda b, c: (b * nc + c, col))
    st = pl.BlockSpec((1, HB_HEADS, HB_DIM, HB_DIM), lambda b, c: (b, 0, 0, 0))
    return pl.pallas_call(
        _hgrn_kernel,
        grid=(batch, nc),
        in_specs=[blk(4), blk(5), blk(6), blk(7),
                  pl.BlockSpec((None, 1, w), lambda b, c: (layer, 0, 0)),
                  pl.BlockSpec((None, 1, w), lambda b, c: (layer, 0, 0)),
                  pl.BlockSpec((None, 1, HB_HEADS, HB_DIM, HB_DIM), lambda b, c: (layer, b, 0, 0, 0))],
        out_specs=[pl.BlockSpec((rows, w), lambda b, c: (b * nc + c, 0)), st],
        out_shape=[jax.ShapeDtypeStruct((batch * seq, w), BF16),
                   jax.ShapeDtypeStruct(s0.shape[1:], F32)],
        scratch_shapes=[pltpu.VMEM((HB_HEADS, HB_DIM, HB_DIM), F32),
                        pltpu.VMEM((L, w), F32), pltpu.VMEM((L, w), F32),
                        pltpu.VMEM((L, w), F32), pltpu.VMEM((rows, w), F32)],
        compiler_params=_params("parallel", "arbitrary"),
        name="hgrn_scan",
    )(z, z, z, z, lb_all, hb_norm, s0)


def _mm_br_kernel(ya_ref, yb_ref, wa_ref, wb_ref, ga_ref, gb_ref, o_ref):
    ta = _dot(ya_ref[...], wa_ref[...])
    tb = _dot(yb_ref[...], wb_ref[...])
    o_ref[...] = (_sigmoid(ga_ref[...]) * ta + _sigmoid(gb_ref[...]) * tb).astype(BF16)


def _mm_branches(ya, yb, wa, wb, z, ga_col, gb_col, layer, tm=1024, tn=1024):
    n, kdim = ya.shape
    d = wa.shape[1]
    tm = min(tm, n)
    tn = _col_tile(tn, n, d, 2048)
    ga_blk, gb_blk = ga_col // tn, gb_col // tn
    return pl.pallas_call(
        _mm_br_kernel,
        grid=(n // tm, d // tn),
        in_specs=[pl.BlockSpec((tm, kdim), lambda i, j: (i, 0)),
                  pl.BlockSpec((tm, kdim), lambda i, j: (i, 0)),
                  pl.BlockSpec((kdim, tn), lambda i, j: (0, j)),
                  pl.BlockSpec((kdim, tn), lambda i, j: (0, j)),
                  pl.BlockSpec((tm, tn), lambda i, j: (i, ga_blk + j)),
                  pl.BlockSpec((tm, tn), lambda i, j: (i, gb_blk + j))],
        out_specs=pl.BlockSpec((tm, tn), lambda i, j: (i, j)),
        out_shape=jax.ShapeDtypeStruct((n, d), BF16),
        compiler_params=_params("parallel", "arbitrary"),
        name="branch_merge",
    )(ya, yb, wa, wb, z, z)


def _mm_res_kernel(a_ref, w_ref, x_ref, gt_ref, o_ref):
    o_ref[...] = x_ref[...] + gt_ref[...] * _dot(a_ref[...], w_ref[...])


def _mm_residual(a, w, x, mod, gate_chunk, layer, rows_per_stream, tm=2048, tn=512):
    n, kdim = a.shape
    d = w.shape[1]
    tm, tps = _row_tile(tm, n, mod, rows_per_stream)
    tn = _col_tile(tn, n, d, 2048)
    nj = d // tn
    if mod.ndim == 3:
        gate_spec = pl.BlockSpec((None, 1, tn), lambda i, j: (i // tps, 0, gate_chunk * nj + j))
    else:
        gate_spec = pl.BlockSpec((tm, tn), lambda i, j: (i, gate_chunk * nj + j))
    return pl.pallas_call(
        _mm_res_kernel,
        grid=(n // tm, nj),
        in_specs=[pl.BlockSpec((tm, kdim), lambda i, j: (i, 0)),
                  pl.BlockSpec((kdim, tn), lambda i, j: (0, j)),
                  pl.BlockSpec((tm, tn), lambda i, j: (i, j)),
                  gate_spec],
        out_specs=pl.BlockSpec((tm, tn), lambda i, j: (i, j)),
        out_shape=jax.ShapeDtypeStruct((n, d), F32),
        compiler_params=_params("parallel", "arbitrary"),
        name="out_proj_residual",
    )(a, w, x, mod)


def _ffn_kernel(x_ref, g_ref, sc_ref, sh_ref, gt_ref, wu_ref, wd_ref, fg_ref, o_ref, h_scr, inv_scr,
                *, final):
    f = pl.program_id(1)

    @pl.when(f == 0)
    def _():
        _norm_mod_rows(x_ref, g_ref, sc_ref, sh_ref, h_scr, inv_scr)
        o_ref[...] = jnp.zeros_like(o_ref)

    u = jnp.square(jnp.maximum(_dot(h_scr[...], wu_ref[...]), 0.0))
    o_ref[...] += _dot(u.astype(BF16), wd_ref[...])

    @pl.when(f == pl.num_programs(1) - 1)
    def _():
        tm = x_ref.shape[0]
        rows = min(tm, 256)

        def body(r, carry):
            sl = pl.ds(pl.multiple_of(r * rows, rows), rows)
            gt = gt_ref[...] if gt_ref.shape[0] == 1 else gt_ref[sl, :]
            xn = x_ref[sl, :] + gt * o_ref[sl, :]
            if final:
                xn = xn * lax.rsqrt(jnp.mean(xn * xn, axis=-1, keepdims=True) + EPS) * fg_ref[...]
            o_ref[sl, :] = xn
            return carry

        lax.fori_loop(0, tm // rows, body, 0)


def _ffn(x, g, mod, wu, wd, final_g, final, layer, rows_per_stream, tm=512, tf=2048):
    n, d = x.shape
    dff = wu.shape[1]
    tm, tps = _row_tile(tm, n, mod, rows_per_stream)
    tf = _col_tile(tf, n, dff, 2048)
    return pl.pallas_call(
        functools.partial(_ffn_kernel, final=final),
        grid=(n // tm, dff // tf),
        in_specs=[pl.BlockSpec((tm, d), lambda i, j: (i, 0)),
                  pl.BlockSpec((None, 1, d), lambda i, j: (layer, 0, 0)),
                  _mod_spec(mod, 4, d, tm, tps),
                  _mod_spec(mod, 3, d, tm, tps),
                  _mod_spec(mod, 5, d, tm, tps),
                  pl.BlockSpec((d, tf), lambda i, j: (0, j)),
                  pl.BlockSpec((tf, d), lambda i, j: (j, 0)),
                  pl.BlockSpec((1, d), lambda i, j: (0, 0))],
        out_specs=pl.BlockSpec((tm, d), lambda i, j: (i, 0)),
        out_shape=jax.ShapeDtypeStruct((n, d), F32),
        scratch_shapes=[pltpu.VMEM((tm, d), BF16), pltpu.VMEM((tm, 1), F32)],
        compiler_params=_params("parallel", "arbitrary"),
        name="ffn_residual",
    )(x, g, mod, mod, mod, wu, wd, final_g)


def _trunk(x, mod_all, per_token_mod, conv_c, st_c, st_n, st_m, st_s, lb_all, wts, mats=None):
    cast_mats = []
    batch, seq, d = x.shape
    depth = conv_c.shape[0]
    L = min(MA_CHUNK, seq)
    Lh = min(HB_CHUNK, seq)
    nc = seq // L
    n = batch * seq
    nz = wts["b_in"].shape[2]
    keep = CONV_W - 1
    xf = x.reshape(n, d)
    bufs, cms, nvs, mrs, sms = [], [], [], [], []
    for l in range(depth):
        mod = mod_all[l]
        if per_token_mod:
            mod = jnp.repeat(mod, seq, axis=0)
        else:
            mod = mod[:, None, :]
        h, zg = _norm_gates(xf, wts["norm1_g"], mod, wts["w_gate"], wts["b_gate"], l, seq)
        if mats is None:
            z, layer_mats = _mm_in(h, wts["w_in_t"], wts["b_in"], l, nz, side=[(w, l) for w in wts["mats_f32"]])
            cast_mats.append(layer_mats)
        else:
            layer_mats = mats[l]
            z, _ = _mm_in(h, wts["w_in_t"], wts["b_in"], l, nz)
        w_br_a, w_br_b, w_o, w_up, w_down = layer_mats
        gates_t = jnp.swapaxes(zg[:, :2 * MA_HEADS].reshape(batch, nc, L, 2 * MA_HEADS), 2, 3)
        conv0 = jnp.pad(conv_c[l], ((0, 0), (CONV_PAD - keep, 0), (0, 0)))
        m0 = jnp.broadcast_to(st_m[l][:, :, None], (batch, MA_HEADS, GATE_PAD))
        ya, c1, n1, m1 = _mlstm(z, zg, gates_t, wts["conv_w"], wts["conv_b"], wts["ma_norm"],
                                conv0, st_c, st_n, m0, l, batch, seq, L)
        yb, s1 = _hgrn(z, lb_all, wts["hb_norm"], st_s, l, batch, seq, Lh)
        merged = _mm_branches(ya, yb, w_br_a, w_br_b, z,
                              4 * MA_WIDTH + 4 * HB_WIDTH, 4 * MA_WIDTH + 4 * HB_WIDTH + d, l)
        xf = _mm_residual(merged, w_o, xf, mod, 2, l, seq)
        xf = _ffn(xf, wts["norm2_g"], mod, w_up, w_down, wts["final_g"], l == depth - 1, l, seq)
        bufs.append(z.reshape(batch, seq, nz)[:, seq - keep:, :2 * MA_WIDTH])
        cms.append(c1)
        nvs.append(n1)
        mrs.append(m1[:, :, 0])
        sms.append(s1)
    outs = (xf.reshape(batch, seq, d), jnp.stack(bufs), jnp.stack(cms), jnp.stack(nvs),
            jnp.stack(mrs), jnp.stack(sms))
    return outs, (cast_mats if mats is None else mats)


def kernel(x_prompt, x_sample, cache_conv, state_mlstm_C, state_mlstm_n, state_mlstm_m, state_hgrn,
           c_prompt, c_sample, ada_w, ada_b, norm1_g, norm2_g, w_in, b_in, conv_w, conv_b, ma_norm,
           hgrn_lb_raw, hb_norm, w_br_a, w_br_b, w_o, w_up, w_down, final_g):
    depth, d, n_in = w_in.shape
    bp, bs = x_prompt.shape[0], x_sample.shape[0]
    n_main = n_in - 2 * MA_HEADS
    gate_pad = GATE_PAD - 2 * MA_HEADS

    lb_all = _lower_bounds(hgrn_lb_raw).reshape(depth, 1, HB_WIDTH)
    c_all = jnp.concatenate([c_prompt, c_sample], axis=0)
    c_rows = -(-c_all.shape[0] // 8) * 8
    c_all = jnp.pad(c_all, ((0, c_rows - c_all.shape[0]), (0, 0)))
    mod_all = _modulation(c_all, ada_w, ada_b)

    w_in_t = jnp.swapaxes(w_in, 1, 2)
    wts = {
        "norm1_g": norm1_g.reshape(depth, 1, d),
        "norm2_g": norm2_g.reshape(depth, 1, d),
        "w_in_t": w_in_t.astype(BF16),
        "b_in": b_in[:, :n_main].reshape(depth, 1, n_main),
        "w_gate": jnp.pad(jnp.swapaxes(w_in_t[:, n_main:, :], 1, 2),
                          ((0, 0), (0, 0), (0, gate_pad))).astype(BF16),
        "b_gate": jnp.pad(b_in[:, n_main:], ((0, 0), (0, gate_pad))).reshape(depth, 1, GATE_PAD),
        "conv_w": conv_w,
        "conv_b": conv_b.reshape(depth, 1, -1),
        "ma_norm": ma_norm.reshape(depth, 1, -1),
        "hb_norm": hb_norm.reshape(depth, 1, -1),
        "mats_f32": (w_br_a, w_br_b, w_o, w_up, w_down),
        "final_g": final_g.reshape(1, d),
    }

    f32 = jnp.float32
    z_conv = jnp.zeros((depth, bp, CONV_W - 1, 2 * MA_WIDTH), f32)
    z_c = jnp.zeros((depth, bp, MA_HEADS, MA_HEAD_DIM, MA_HEAD_DIM), f32)
    z_n = jnp.zeros((depth, bp, MA_HEADS, MA_HEAD_DIM), f32)
    z_m = jnp.zeros((depth, bp, MA_HEADS), f32)
    z_s = jnp.zeros((depth, bp, HB_HEADS, HB_DIM, HB_DIM), f32)

    out_p, mats = _trunk(x_prompt, mod_all[:, :bp], False, z_conv, z_c, z_n, z_m, z_s, lb_all, wts)
    out_s, _ = _trunk(x_sample, mod_all[:, bp:bp + bs], True, cache_conv, state_mlstm_C, state_mlstm_n,
                      state_mlstm_m, state_hgrn, lb_all, wts, mats)
    y_p, conv_p, c_p, n_p, m_p, s_p = out_p
    y_s, conv_s, c_s, n_s, m_s, s_s = out_s
    return (y_p, y_s, conv_p, c_p, n_p, m_p, s_p, conv_s, c_s, n_s, m_s, s_s)
```
